```python
import math
import jax, jax.numpy as jnp
from jax import lax
import numpy as np

D_MODEL = 1024
BATCH = 16
SEQ = 2048
DEPTH = 2
DEC_BATCH = 8
DEC_SEQ = 4096
PAST_LEN = 128

GRID_W = 64
HEAD_DIM = 64
NA_HEADS = D_MODEL // (2 * HEAD_DIM)
GQA_HEADS = D_MODEL // (2 * HEAD_DIM)
GQA_KV_HEADS = GQA_HEADS // 4
DIFF_HEADS = D_MODEL // (2 * HEAD_DIM)
NA_WIN_H = 8
NA_WIN_W = 16
Q_BLOCK = 128
ROPE_THETA = 10000.0
D_FF = -(-8 * D_MODEL // (3 * 256)) * 256
NA_W = NA_HEADS * HEAD_DIM
GQ_W = GQA_HEADS * HEAD_DIM
GKV_W = GQA_KV_HEADS * HEAD_DIM
IN0_W = 3 * NA_W + GQ_W + 2 * GKV_W
IN1_W = 3 * D_MODEL
LN_EPS = 1e-5
RMS_EPS = 1e-6
SUBLN_EPS = 1e-5

kernel_name = 'hybrid_na_gqa_diffattn_encoder'


def layer_norm(x, g, b):
    xf = x.astype(jnp.float32)
    mu = jnp.mean(xf, -1, keepdims=True)
    var = jnp.mean(jnp.square(xf - mu), -1, keepdims=True)
    return ((xf - mu) * lax.rsqrt(var + LN_EPS) * g + b).astype(x.dtype)


def rms_norm(x, g, eps):
    xf = x.astype(jnp.float32)
    return (xf * lax.rsqrt(jnp.mean(jnp.square(xf), -1, keepdims=True) + eps) * g).astype(x.dtype)


def rope_cos_sin(pos, dim):
    inv = ROPE_THETA ** (-jnp.arange(0, dim, 2, dtype=jnp.float32) / dim)
    ang = pos.astype(jnp.float32)[:, None] * inv[None, :]
    return jnp.cos(ang), jnp.sin(ang)


def apply_rope(x, cos, sin):
    xf = x.astype(jnp.float32)
    x1, x2 = jnp.split(xf, 2, axis=-1)
    c = cos[:, None, :]
    s = sin[:, None, :]
    return jnp.concatenate([x1 * c - x2 * s, x1 * s + x2 * c], axis=-1).astype(x.dtype)


def apply_axial_rope(x, row_cs, col_cs):
    half = x.shape[-1] // 2
    return jnp.concatenate([apply_rope(x[..., :half], *row_cs), apply_rope(x[..., half:], *col_cs)], axis=-1)


def sweep_query_blocks(fn, q):
    B, L = q.shape[0], q.shape[1]
    nb = L // Q_BLOCK
    qb = jnp.moveaxis(q.reshape((B, nb, Q_BLOCK) + q.shape[2:]), 1, 0)
    out = lax.map(fn, qb)
    return jnp.moveaxis(out, 0, 1).reshape((B, L) + out.shape[3:])


def neighbourhood_attention(q, k, v, rpb):
    B, L, H, dh = q.shape
    rows = L // GRID_W
    kh = min(NA_WIN_H, rows)
    kw = NA_WIN_W
    qg = (q * (dh ** -0.5)).reshape(B, rows, GRID_W, H, dh)
    kg = k.reshape(B, rows, GRID_W, H, dh)
    vg = v.reshape(B, rows, GRID_W, H, dh)
    cols = jnp.arange(GRID_W)
    col_start = jnp.clip(cols - kw // 2, 0, GRID_W - kw)
    col_idx = col_start[:, None] + jnp.arange(kw)[None, :]
    col_bias_idx = col_idx - cols[:, None] + (NA_WIN_W - 1)

    def row_block(args):
        r, q_row = args
        r0 = jnp.clip(r - kh // 2, 0, rows - kh)
        k_rows = lax.dynamic_slice_in_dim(kg, r0, kh, axis=1)
        v_rows = lax.dynamic_slice_in_dim(vg, r0, kh, axis=1)
        k_win = jnp.take(k_rows, col_idx, axis=2)
        v_win = jnp.take(v_rows, col_idx, axis=2)
        s = jnp.einsum('bqhd,bjqkhd->bhqjk', q_row, k_win).astype(jnp.float32)
        row_bias_idx = r0 + jnp.arange(kh) - r + (NA_WIN_H - 1)
        bias = rpb[:, row_bias_idx[:, None, None], col_bias_idx[None, :, :]]
        s = s + jnp.transpose(bias, (0, 2, 1, 3))[None].astype(jnp.float32)
        p = jax.nn.softmax(s.reshape(B, H, GRID_W, kh * kw), axis=-1)
        p = p.reshape(B, H, GRID_W, kh, kw).astype(v.dtype)
        return jnp.einsum('bhqjk,bjqkhd->bqhd', p, v_win)

    out = lax.map(row_block, (jnp.arange(rows), jnp.moveaxis(qg, 1, 0)))
    return jnp.moveaxis(out, 0, 1).reshape(B, L, H * dh)


def gqa_axial_attention(q, k, v, g_q, g_k, row_cs, col_cs):
    B, L = q.shape[0], q.shape[1]
    q = apply_axial_rope(rms_norm(q, g_q, RMS_EPS), row_cs, col_cs)
    k = apply_axial_rope(rms_norm(k, g_k, RMS_EPS), row_cs, col_cs)
    group = GQA_HEADS // GQA_KV_HEADS
    q = (q * (HEAD_DIM ** -0.5)).reshape(B, L, GQA_KV_HEADS, group, HEAD_DIM)

    def block(qb):
        s = jnp.einsum('bqkgd,bskd->bkgqs', qb, k).astype(jnp.float32)
        p = jax.nn.softmax(s, axis=-1).astype(v.dtype)
        return jnp.einsum('bkgqs,bskd->bqkgd', p, v)

    return sweep_query_blocks(block, q).reshape(B, L, GQ_W)


def differential_attention(q, k, v, lq1, lk1, lq2, lk2, g_subln, lambda_init, seq_cs):
    B, L = q.shape[0], q.shape[1]
    q = apply_rope(q, *seq_cs)
    k = apply_rope(k, *seq_cs)
    q = (q * (HEAD_DIM ** -0.5)).reshape(B, L, DIFF_HEADS, 2, HEAD_DIM)
    k = k.reshape(B, L, DIFF_HEADS, 2, HEAD_DIM)
    lam = (jnp.exp(jnp.sum(lq1.astype(jnp.float32) * lk1.astype(jnp.float32)))
           - jnp.exp(jnp.sum(lq2.astype(jnp.float32) * lk2.astype(jnp.float32))) + lambda_init)

    def block(qb):
        s = jnp.einsum('bqhcd,bshcd->bhcqs', qb, k).astype(jnp.float32)
        p = jax.nn.softmax(s, axis=-1)
        a = (p[:, :, 0] - lam * p[:, :, 1]).astype(v.dtype)
        return jnp.einsum('bhqs,bshe->bqhe', a, v)

    o = sweep_query_blocks(block, q)
    o = rms_norm(o, g_subln, SUBLN_EPS) * (1.0 - lambda_init)
    return o.reshape(B, L, D_MODEL)


def na_gqa_mixer(x, w_in, rpb, g_q, g_k, w_out, row_cs, col_cs):
    B, L, _ = x.shape
    h = x @ w_in
    cuts = [NA_W, 2 * NA_W, 3 * NA_W, 3 * NA_W + GQ_W, 3 * NA_W + GQ_W + GKV_W]
    na_q, na_k, na_v, gq, gk, gv = jnp.split(h, cuts, axis=-1)
    a_out = neighbourhood_attention(na_q.reshape(B, L, NA_HEADS, HEAD_DIM),
                                    na_k.reshape(B, L, NA_HEADS, HEAD_DIM),
                                    na_v.reshape(B, L, NA_HEADS, HEAD_DIM), rpb)
    b_out = gqa_axial_attention(gq.reshape(B, L, GQA_HEADS, HEAD_DIM),
                                gk.reshape(B, L, GQA_KV_HEADS, HEAD_DIM),
                                gv.reshape(B, L, GQA_KV_HEADS, HEAD_DIM), g_q, g_k, row_cs, col_cs)
    return jnp.concatenate([a_out, b_out], axis=-1) @ w_out


def diff_mixer(x, w_in, lq1, lk1, lq2, lk2, g_subln, w_out, lambda_init, seq_cs):
    B, L, _ = x.shape
    q, k, v = jnp.split(x @ w_in, 3, axis=-1)
    o = differential_attention(q.reshape(B, L, 2 * DIFF_HEADS, HEAD_DIM),
                               k.reshape(B, L, 2 * DIFF_HEADS, HEAD_DIM),
                               v.reshape(B, L, DIFF_HEADS, 2 * HEAD_DIM),
                               lq1, lk1, lq2, lk2, g_subln, lambda_init, seq_cs)
    return o @ w_out


def swiglu(x, wg, wu, wd):
    return (jax.nn.silu(x @ wg) * (x @ wu)) @ wd


def encoder_trunk(x, w_in_mix0, rpb_na, g_q_gqa, g_k_gqa, w_out_mix0,
                  w_in_mix1, lam_q1, lam_k1, lam_q2, lam_k2, g_subln, w_out_mix1,
                  ln_mix_g, ln_mix_b, w_ffn_gate, w_ffn_up, w_ffn_down, ln_ffn_g, ln_ffn_b):
    L = x.shape[1]
    t = jnp.arange(L)
    row_cs = rope_cos_sin(t // GRID_W, HEAD_DIM // 2)
    col_cs = rope_cos_sin(t % GRID_W, HEAD_DIM // 2)
    seq_cs = rope_cos_sin(t, HEAD_DIM)
    alpha = (2.0 * DEPTH) ** 0.25
    for i in range(DEPTH):
        j = i // 2
        if i % 2 == 0:
            mix = na_gqa_mixer(x, w_in_mix0[j], rpb_na[j], g_q_gqa[j], g_k_gqa[j], w_out_mix0[j], row_cs, col_cs)
        else:
            lambda_init = 0.8 - 0.6 * math.exp(-0.3 * i)
            mix = diff_mixer(x, w_in_mix1[j], lam_q1[j], lam_k1[j], lam_q2[j], lam_k2[j], g_subln[j],
                             w_out_mix1[j], lambda_init, seq_cs)
        x = layer_norm(alpha * x + mix, ln_mix_g[i], ln_mix_b[i])
        x = layer_norm(alpha * x + swiglu(x, w_ffn_gate[i], w_ffn_up[i], w_ffn_down[i]), ln_ffn_g[i], ln_ffn_b[i])
    return x


def setup_inputs(seed: int = 0) -> dict:
    key = jax.random.key(seed)
    ks = jax.random.split(key, 24)
    n_even = (DEPTH + 1) // 2
    n_odd = DEPTH // 2
    beta = (8.0 * DEPTH) ** -0.25
    f32 = jnp.float32
    nrm = lambda k, shape, s: jax.random.normal(k, shape, f32) * s
    return {
        'x_prompt': nrm(ks[0], (BATCH, SEQ, D_MODEL), 1.0),
        'x_sample': nrm(ks[1], (DEC_BATCH, DEC_SEQ, D_MODEL), 1.0),
        'w_in_mix0': nrm(ks[2], (n_even, D_MODEL, IN0_W), D_MODEL ** -0.5),
        'rpb_na': nrm(ks[3], (n_even, NA_HEADS, 2 * NA_WIN_H - 1, 2 * NA_WIN_W - 1), 0.1),
        'g_q_gqa': 1.0 + nrm(ks[4], (n_even, HEAD_DIM), 0.1),
        'g_k_gqa': 1.0 + nrm(ks[5], (n_even, HEAD_DIM), 0.1),
        'w_out_mix0': nrm(ks[6], (n_even, D_MODEL, D_MODEL), beta * D_MODEL ** -0.5),
        'w_in_mix1': nrm(ks[7], (n_odd, D_MODEL, IN1_W), D_MODEL ** -0.5),
        'lam_q1': nrm(ks[8], (n_odd, HEAD_DIM), 0.1),
        'lam_k1': nrm(ks[9], (n_odd, HEAD_DIM), 0.1),
        'lam_q2': nrm(ks[10], (n_odd, HEAD_DIM), 0.1),
        'lam_k2': nrm(ks[11], (n_odd, HEAD_DIM), 0.1),
        'g_subln': 1.0 + nrm(ks[12], (n_odd, 2 * HEAD_DIM), 0.1),
        'w_out_mix1': nrm(ks[13], (n_odd, D_MODEL, D_MODEL), beta * D_MODEL ** -0.5),
        'ln_mix_g': 1.0 + nrm(ks[14], (DEPTH, D_MODEL), 0.05),
        'ln_mix_b': nrm(ks[15], (DEPTH, D_MODEL), 0.02),
        'w_ffn_gate': nrm(ks[16], (DEPTH, D_MODEL, D_FF), D_MODEL ** -0.5),
        'w_ffn_up': nrm(ks[17], (DEPTH, D_MODEL, D_FF), D_MODEL ** -0.5),
        'w_ffn_down': nrm(ks[18], (DEPTH, D_FF, D_MODEL), beta * D_FF ** -0.5),
        'ln_ffn_g': 1.0 + nrm(ks[19], (DEPTH, D_MODEL), 0.05),
        'ln_ffn_b': nrm(ks[20], (DEPTH, D_MODEL), 0.02),
    }


def reference(x_prompt, x_sample, w_in_mix0, rpb_na, g_q_gqa, g_k_gqa, w_out_mix0,
              w_in_mix1, lam_q1, lam_k1, lam_q2, lam_k2, g_subln, w_out_mix1,
              ln_mix_g, ln_mix_b, w_ffn_gate, w_ffn_up, w_ffn_down, ln_ffn_g, ln_ffn_b):
    y_prompt = encoder_trunk(x_prompt, w_in_mix0, rpb_na, g_q_gqa, g_k_gqa, w_out_mix0,
                             w_in_mix1, lam_q1, lam_k1, lam_q2, lam_k2, g_subln, w_out_mix1,
                             ln_mix_g, ln_mix_b, w_ffn_gate, w_ffn_up, w_ffn_down, ln_ffn_g, ln_ffn_b)
    y_sample = encoder_trunk(x_sample, w_in_mix0, rpb_na, g_q_gqa, g_k_gqa, w_out_mix0,
                             w_in_mix1, lam_q1, lam_k1, lam_q2, lam_k2, g_subln, w_out_mix1,
                             ln_mix_g, ln_mix_b, w_ffn_gate, w_ffn_up, w_ffn_down, ln_ffn_g, ln_ffn_b)
    return (y_prompt, y_sample)
```

```python
import functools
import math

import jax
import jax.numpy as jnp
from jax import lax
from jax.experimental import pallas as pl
from jax.experimental.pallas import tpu as pltpu

D_MODEL = 1024
DEPTH = 2
GRID_W = 64
HEAD_DIM = 64
NA_HEADS = 8
GQA_HEADS = 8
GQA_KV_HEADS = 2
DIFF_HEADS = 8
NA_WIN_H = 8
NA_WIN_W = 16
ROPE_THETA = 10000.0
D_FF = 2816
LN_EPS = 1e-5
RMS_EPS = 1e-6
SUBLN_EPS = 1e-5
ALPHA = (2.0 * DEPTH) ** 0.25
Q_SCALE = HEAD_DIM ** -0.5

LANES = 128
FF_CHUNK = 256
N_FF_CHUNKS = D_FF // FF_CHUNK
ROW_TILE = 512
ATT_TQ = 512
ATT_TK = 512
MASK_VALUE = -1e30
VMEM_LIMIT = 56 * 1024 * 1024

_BF16 = jnp.bfloat16
_F32 = jnp.float32
_NT = (((1,), (1,)), ((), ()))


def _params(*semantics):
    return pltpu.CompilerParams(dimension_semantics=semantics, vmem_limit_bytes=VMEM_LIMIT)


def _const_spec(shape):
    zeros = (0,) * len(shape)
    return pl.BlockSpec(shape, lambda *_: zeros, pipeline_mode=pl.Buffered(1))


def _layer_norm(xf, gain, bias):
    mu = jnp.mean(xf, axis=-1, keepdims=True)
    xc = xf - mu
    var = jnp.mean(xc * xc, axis=-1, keepdims=True)
    return xc * lax.rsqrt(var + LN_EPS) * gain + bias


def _rotate_pairs(x, half):
    lane = lax.broadcasted_iota(jnp.int32, x.shape, 1)
    first = (lane % (2 * half)) < half
    return jnp.where(first, pltpu.roll(x, LANES - half, 1), pltpu.roll(x, half, 1))


H0_W = 3072


def _proj0_kernel(x_ref, wa_ref, wb_ref, cos_ref, sin_ref, gain_ref, ones_ref, o_ref):
    xb = x_ref[...].astype(_BF16)
    ha = jnp.dot(xb, wa_ref[...], preferred_element_type=_F32)
    o_ref[:, 0:512] = (ha[:, 0:512] * Q_SCALE).astype(_BF16)
    o_ref[:, 512:1536] = ha[:, 512:1536].astype(_BF16)

    hb = jnp.dot(xb, wb_ref[...], preferred_element_type=_F32)
    cos = cos_ref[...]
    sin = sin_ref[...]
    ones_bd = ones_ref[...]
    lo64 = lax.broadcasted_iota(jnp.int32, cos.shape, 1) < HEAD_DIM

    def norm_rope(xg, gain):
        ssq = jnp.dot((xg * xg).astype(_BF16), ones_bd, preferred_element_type=_F32)
        n = xg * lax.rsqrt(ssq * (1.0 / HEAD_DIM) + RMS_EPS) * gain
        return n * cos + _rotate_pairs(n, 16) * sin

    for g in range(4):
        sl = slice(LANES * g, LANES * (g + 1))
        qg = norm_rope(hb[:, sl], gain_ref[:, sl]) * Q_SCALE
        o_ref[:, 1536 + LANES * g:1536 + LANES * (g + 1)] = qg.astype(_BF16)

    def expand(t, base):
        tr = pltpu.roll(t, HEAD_DIM, 1)
        zero = jnp.zeros_like(t)
        parts = (jnp.where(lo64, t, zero), jnp.where(lo64, zero, tr),
                 jnp.where(lo64, tr, zero), jnp.where(lo64, zero, t))
        for i, part in enumerate(parts):
            o_ref[:, base + LANES * i:base + LANES * (i + 1)] = part.astype(_BF16)

    expand(norm_rope(hb[:, 512:640], gain_ref[:, 512:640]), 2048)
    expand(hb[:, 640:768], 2560)


def _proj0(x2d, wa, wb, cos, sin, gain, ones_bd, seq_len):
    rows = x2d.shape[0]
    tm = ROW_TILE
    pos_blocks = seq_len // tm
    return pl.pallas_call(
        _proj0_kernel,
        grid=(rows // tm,),
        in_specs=[
            pl.BlockSpec((tm, D_MODEL), lambda i: (i, 0)),
            _const_spec(wa.shape),
            _const_spec(wb.shape),
            pl.BlockSpec((tm, LANES), lambda i: (i % pos_blocks, 0)),
            pl.BlockSpec((tm, LANES), lambda i: (i % pos_blocks, 0)),
            _const_spec(gain.shape),
            _const_spec(ones_bd.shape),
        ],
        out_specs=pl.BlockSpec((tm, H0_W), lambda i: (i, 0)),
        out_shape=jax.ShapeDtypeStruct((rows, H0_W), _BF16),
        compiler_params=_params("parallel"),
        name="proj0",
    )(x2d, wa, wb, cos, sin, gain, ones_bd)


H1_W = 3 * D_MODEL


def _proj1_kernel(x_ref, w_ref, cos_ref, sin_ref, o_ref):
    xb = x_ref[...].astype(_BF16)
    cos = cos_ref[...]
    sin = sin_ref[...]
    for part, scale in ((0, Q_SCALE), (1, 1.0)):
        base = part * D_MODEL
        h = jnp.dot(xb, w_ref[:, base:base + D_MODEL], preferred_element_type=_F32)
        for g in range(D_MODEL // LANES):
            xg = h[:, LANES * g:LANES * (g + 1)]
            r = xg * cos + _rotate_pairs(xg, 32) * sin
            if scale != 1.0:
                r = r * scale
            o_ref[:, base + LANES * g:base + LANES * (g + 1)] = r.astype(_BF16)
    hv = jnp.dot(xb, w_ref[:, 2 * D_MODEL:], preferred_element_type=_F32)
    o_ref[:, 2 * D_MODEL:] = hv.astype(_BF16)


def _proj1(x2d, w, cos, sin, seq_len):
    rows = x2d.shape[0]
    tm = ROW_TILE
    pos_blocks = seq_len // tm
    return pl.pallas_call(
        _proj1_kernel,
        grid=(rows // tm,),
        in_specs=[
            pl.BlockSpec((tm, D_MODEL), lambda i: (i, 0)),
            _const_spec(w.shape),
            pl.BlockSpec((tm, LANES), lambda i: (i % pos_blocks, 0)),
            pl.BlockSpec((tm, LANES), lambda i: (i % pos_blocks, 0)),
        ],
        out_specs=pl.BlockSpec((tm, H1_W), lambda i: (i, 0)),
        out_shape=jax.ShapeDtypeStruct((rows, H1_W), _BF16),
        compiler_params=_params("parallel"),
        name="proj1",
    )(x2d, w, cos, sin)


def _softmax_step(s, m_prev, l_prev):
    m_next = jnp.maximum(m_prev, jnp.max(s, axis=1, keepdims=True))
    alpha = jnp.exp(m_prev - m_next)
    p = jnp.exp(s - m_next)
    l_next = alpha * l_prev + jnp.sum(p, axis=1, keepdims=True)
    return p, alpha, m_next, l_next


def _gqa_kernel(q_ref, k_ref, v_ref, o_ref, acc_ref, *, seq_len):
    tq, tk = ATT_TQ, ATT_TK
    n_q, n_k = seq_len // tq, seq_len // tk
    group = GQA_HEADS // GQA_KV_HEADS

    def q_body(qi, carry):
        qs = pl.multiple_of(qi * tq, tq)
        for kvh in range(GQA_KV_HEADS):
            q_pairs = [q_ref[pl.ds(qs, tq), LANES * (2 * kvh + p):LANES * (2 * kvh + p + 1)] for p in range(2)]
            acc_ref[...] = jnp.zeros_like(acc_ref)

            def k_body(kc, state):
                ks = pl.multiple_of(kc * tk, tk)
                new_state = []
                for j in range(group):
                    col = 2 * LANES * kvh + LANES * (j % 2)
                    kk = k_ref[pl.ds(ks, tk), col:col + LANES]
                    vv = v_ref[pl.ds(ks, tk), col:col + LANES]
                    s = lax.dot_general(q_pairs[j // 2], kk, _NT, preferred_element_type=_F32)
                    p, alpha, m_next, l_next = _softmax_step(s, state[2 * j], state[2 * j + 1])
                    pv = jnp.dot(p.astype(_BF16), vv, preferred_element_type=_F32)
                    acc_ref[j] = alpha * acc_ref[j] + pv
                    new_state += [m_next, l_next]
                return tuple(new_state)

            init = (jnp.full((tq, 1), MASK_VALUE, _F32), jnp.zeros((tq, 1), _F32)) * group
            state = lax.fori_loop(0, n_k, k_body, init)
            for p in range(2):
                j0, j1 = 2 * p, 2 * p + 1
                out = acc_ref[j0] * (1.0 / state[2 * j0 + 1]) + acc_ref[j1] * (1.0 / state[2 * j1 + 1])
                o_ref[pl.ds(qs, tq), LANES * (2 * kvh + p):LANES * (2 * kvh + p + 1)] = out.astype(_BF16)
        return carry

    lax.fori_loop(0, n_q, q_body, 0)


def _gqa_attention(h0, batch, seq_len):
    h3 = h0.reshape(batch, seq_len, H0_W)
    width = GQA_HEADS * HEAD_DIM
    return pl.pallas_call(
        functools.partial(_gqa_kernel, seq_len=seq_len),
        grid=(batch,),
        in_specs=[
            pl.BlockSpec((None, seq_len, width), lambda b: (b, 0, 3)),
            pl.BlockSpec((None, seq_len, width), lambda b: (b, 0, 4)),
            pl.BlockSpec((None, seq_len, width), lambda b: (b, 0, 5)),
        ],
        out_specs=pl.BlockSpec((None, seq_len, width), lambda b: (b, 0, 0)),
        out_shape=jax.ShapeDtypeStruct((batch, seq_len, width), _BF16),
        scratch_shapes=[pltpu.VMEM((GQA_HEADS // GQA_KV_HEADS, ATT_TQ, LANES), _F32)],
        compiler_params=_params("parallel"),
        name="gqa_attention",
    )(h3, h3, h3)


NA_KEYS = NA_WIN_H * GRID_W


def _na_kernel(q_ref, k_ref, v_ref, bias_ref, o_ref, *, n_rows):
    lo64 = lax.broadcasted_iota(jnp.int32, (NA_KEYS, LANES), 1) < HEAD_DIM
    out_lo64 = lax.broadcasted_iota(jnp.int32, (GRID_W, LANES), 1) < HEAD_DIM

    def row_body(r, carry):
        r0 = jnp.clip(r - NA_WIN_H // 2, 0, n_rows - NA_WIN_H)
        qs = pl.multiple_of(r * GRID_W, GRID_W)
        ks = pl.multiple_of(r0 * GRID_W, GRID_W)
        q = q_ref[pl.ds(qs, GRID_W), :]
        kslab = k_ref[pl.ds(ks, NA_KEYS), :]
        vslab = v_ref[pl.ds(ks, NA_KEYS), :]
        dr0 = r0 - r + (NA_WIN_H - 1)
        zero = jnp.zeros_like(kslab)
        outs = []
        for e in range(2):
            kk = jnp.where(lo64, kslab, zero) if e == 0 else jnp.where(lo64, zero, kslab)
            s = lax.dot_general(q, kk, _NT, preferred_element_type=_F32) + bias_ref[e, dr0]
            m = jnp.max(s, axis=1, keepdims=True)
            p = jnp.exp(s - m)
            l = jnp.sum(p, axis=1, keepdims=True)
            pv = jnp.dot(p.astype(_BF16), vslab, preferred_element_type=_F32)
            outs.append(pv * (1.0 / l))
        o_ref[pl.ds(qs, GRID_W), :] = jnp.where(out_lo64, outs[0], outs[1]).astype(_BF16)
        return carry

    lax.fori_loop(0, n_rows, row_body, 0, unroll=2)


def _na_attention(h0, bias, batch, seq_len):
    h3 = h0.reshape(batch, seq_len, H0_W)
    n_pairs = NA_HEADS // 2
    return pl.pallas_call(
        functools.partial(_na_kernel, n_rows=seq_len // GRID_W),
        grid=(n_pairs, batch),
        in_specs=[
            pl.BlockSpec((None, seq_len, LANES), lambda g, b: (b, 0, g)),
            pl.BlockSpec((None, seq_len, LANES), lambda g, b: (b, 0, n_pairs + g)),
            pl.BlockSpec((None, seq_len, LANES), lambda g, b: (b, 0, 2 * n_pairs + g)),
            pl.BlockSpec((2, NA_WIN_H, GRID_W, NA_KEYS), lambda g, b: (g, 0, 0, 0)),
        ],
        out_specs=pl.BlockSpec((None, seq_len, LANES), lambda g, b: (b, 0, g)),
        out_shape=jax.ShapeDtypeStruct((batch, seq_len, NA_HEADS * HEAD_DIM), _BF16),
        compiler_params=_params("parallel", "parallel"),
        name="na_attention",
    )(h3, h3, h3, bias)


def _na_bias_table(rpb):
    cols = jnp.arange(GRID_W)
    col_start = jnp.clip(cols - NA_WIN_W // 2, 0, GRID_W - NA_WIN_W)
    kc = jnp.arange(GRID_W)
    in_window = (kc[None, :] >= col_start[:, None]) & (kc[None, :] < col_start[:, None] + NA_WIN_W)
    col_idx = jnp.clip(kc[None, :] - cols[:, None] + (NA_WIN_W - 1), 0, 2 * NA_WIN_W - 2)
    per_row = jnp.where(in_window[None, None], rpb[:, :, col_idx], MASK_VALUE)
    dr = jnp.arange(NA_WIN_H)[:, None] + jnp.arange(NA_WIN_H)[None, :]
    table = per_row[:, dr]
    table = jnp.transpose(table, (0, 1, 3, 2, 4))
    return table.reshape(rpb.shape[0], NA_WIN_H, GRID_W, NA_KEYS).astype(_F32)


def _diff_kernel(q_ref, k_ref, v_ref, lq1_ref, lk1_ref, lq2_ref, lk2_ref, g_ref, o_ref, acc_ref,
                 *, seq_len, lambda_init):
    tq, tk = ATT_TQ, ATT_TK
    n_q, n_k = seq_len // tq, seq_len // tk
    lam = (jnp.exp(jnp.sum(lq1_ref[...] * lk1_ref[...], axis=1, keepdims=True))
           - jnp.exp(jnp.sum(lq2_ref[...] * lk2_ref[...], axis=1, keepdims=True)) + lambda_init)
    lo64 = lax.broadcasted_iota(jnp.int32, (tk, LANES), 1) < HEAD_DIM
    gain = g_ref[...] * (1.0 - lambda_init)

    def q_body(qi, carry):
        qs = pl.multiple_of(qi * tq, tq)
        q = q_ref[pl.ds(qs, tq), :]
        acc_ref[...] = jnp.zeros_like(acc_ref)

        def k_body(kc, state):
            ks = pl.multiple_of(kc * tk, tk)
            kblk = k_ref[pl.ds(ks, tk), :]
            vblk = v_ref[pl.ds(ks, tk), :]
            zero = jnp.zeros_like(kblk)
            new_state = []
            for c in range(2):
                kk = jnp.where(lo64, kblk, zero) if c == 0 else jnp.where(lo64, zero, kblk)
                s = lax.dot_general(q, kk, _NT, preferred_element_type=_F32)
                p, alpha, m_next, l_next = _softmax_step(s, state[2 * c], state[2 * c + 1])
                pv = jnp.dot(p.astype(_BF16), vblk, preferred_element_type=_F32)
                acc_ref[c] = alpha * acc_ref[c] + pv
                new_state += [m_next, l_next]
            return tuple(new_state)

        init = (jnp.full((tq, 1), MASK_VALUE, _F32), jnp.zeros((tq, 1), _F32)) * 2
        state = lax.fori_loop(0, n_k, k_body, init)
        o = acc_ref[0] * (1.0 / state[1]) - lam * (acc_ref[1] * (1.0 / state[3]))
        ms = jnp.mean(o * o, axis=1, keepdims=True)
        o_ref[pl.ds(qs, tq), :] = (o * lax.rsqrt(ms + SUBLN_EPS) * gain).astype(_BF16)
        return carry

    lax.fori_loop(0, n_q, q_body, 0)


def _diff_attention(h1, lq1, lk1, lq2, lk2, g_subln, lambda_init, batch, seq_len):
    h3 = h1.reshape(batch, seq_len, H1_W)
    heads = DIFF_HEADS
    small = lambda n: pl.BlockSpec((1, n), lambda b, h: (0, 0))
    return pl.pallas_call(
        functools.partial(_diff_kernel, seq_len=seq_len, lambda_init=lambda_init),
        grid=(batch, heads),
        in_specs=[
            pl.BlockSpec((None, seq_len, LANES), lambda b, h: (b, 0, h)),
            pl.BlockSpec((None, seq_len, LANES), lambda b, h: (b, 0, heads + h)),
            pl.BlockSpec((None, seq_len, LANES), lambda b, h: (b, 0, 2 * heads + h)),
            small(HEAD_DIM), small(HEAD_DIM), small(HEAD_DIM), small(HEAD_DIM), small(2 * HEAD_DIM),
        ],
        out_specs=pl.BlockSpec((None, seq_len, LANES), lambda b, h: (b, 0, h)),
        out_shape=jax.ShapeDtypeStruct((batch, seq_len, D_MODEL), _BF16),
        scratch_shapes=[pltpu.VMEM((2, ATT_TQ, LANES), _F32)],
        compiler_params=_params("parallel", "parallel"),
        name="diff_attention",
    )(h3, h3, h3, lq1, lk1, lq2, lk2, g_subln)


def _out_ln_kernel(*refs, n_parts):
    part_refs = refs[:n_parts]
    x_ref, w_ref, g_ref, b_ref, o_ref = refs[n_parts:]
    if n_parts == 1:
        a = part_refs[0][...]
    else:
        a = jnp.concatenate([r[...] for r in part_refs], axis=1)
    mix = jnp.dot(a, w_ref[...], preferred_element_type=_F32)
    o_ref[...] = _layer_norm(ALPHA * x_ref[...] + mix, g_ref[...], b_ref[...])


def _out_ln(parts, x2d, w, gain, bias):
    rows = x2d.shape[0]
    tm = ROW_TILE
    part_specs = [pl.BlockSpec((tm, p.shape[1]), lambda i: (i, 0)) for p in parts]
    return pl.pallas_call(
        functools.partial(_out_ln_kernel, n_parts=len(parts)),
        grid=(rows // tm,),
        in_specs=part_specs + [
            pl.BlockSpec((tm, D_MODEL), lambda i: (i, 0)),
            _const_spec(w.shape), _const_spec(gain.shape), _const_spec(bias.shape),
        ],
        out_specs=pl.BlockSpec((tm, D_MODEL), lambda i: (i, 0)),
        out_shape=jax.ShapeDtypeStruct((rows, D_MODEL), _F32),
        compiler_params=_params("parallel"),
        name="out_proj_ln",
    )(*parts, x2d, w, gain, bias)


def _ffn_ln_kernel(x_ref, wg_ref, wu_ref, wd_ref, g_ref, b_ref, o_ref):
    x = x_ref[...]
    xb = x.astype(_BF16)
    hidden = []
    for c in range(N_FF_CHUNKS):
        gate = jnp.dot(xb, wg_ref[c], preferred_element_type=_F32)
        up = jnp.dot(xb, wu_ref[c], preferred_element_type=_F32)
        hidden.append((gate * jax.nn.sigmoid(gate) * up).astype(_BF16))
    mix = jnp.dot(jnp.concatenate(hidden, axis=1), wd_ref[...], preferred_element_type=_F32)
    o_ref[...] = _layer_norm(ALPHA * x + mix, g_ref[...], b_ref[...])


def _ffn_ln(x2d, wg, wu, wd, gain, bias):
    rows = x2d.shape[0]
    tm = ROW_TILE
    return pl.pallas_call(
        _ffn_ln_kernel,
        grid=(rows // tm,),
        in_specs=[
            pl.BlockSpec((tm, D_MODEL), lambda i: (i, 0)),
            _const_spec(wg.shape), _const_spec(wu.shape), _const_spec(wd.shape),
            _const_spec(gain.shape), _const_spec(bias.shape),
        ],
        out_specs=pl.BlockSpec((tm, D_MODEL), lambda i: (i, 0)),
        out_shape=jax.ShapeDtypeStruct((rows, D_MODEL), _F32),
        compiler_params=_params("parallel"),
        name="swiglu_ln",
    )(x2d, wg, wu, wd, gain, bias)


def _rope_tables(angles):
    cos = jnp.cos(angles)
    sin = jnp.sin(angles)
    return jnp.concatenate([cos, cos], axis=1), jnp.concatenate([-sin, sin], axis=1)


def _axial_tables(seq_len):
    t = jnp.arange(seq_len)
    dim = HEAD_DIM // 2
    inv = ROPE_THETA ** (-jnp.arange(0, dim, 2, dtype=_F32) / dim)
    row_c, row_s = _rope_tables((t // GRID_W).astype(_F32)[:, None] * inv[None, :])
    col_c, col_s = _rope_tables((t % GRID_W).astype(_F32)[:, None] * inv[None, :])
    cos = jnp.concatenate([row_c, col_c] * (LANES // HEAD_DIM), axis=1)
    sin = jnp.concatenate([row_s, col_s] * (LANES // HEAD_DIM), axis=1)
    return cos, sin


def _seq_tables(seq_len):
    t = jnp.arange(seq_len)
    inv = ROPE_THETA ** (-jnp.arange(0, HEAD_DIM, 2, dtype=_F32) / HEAD_DIM)
    cos, sin = _rope_tables(t.astype(_F32)[:, None] * inv[None, :])
    reps = LANES // HEAD_DIM
    return jnp.concatenate([cos] * reps, axis=1), jnp.concatenate([sin] * reps, axis=1)


def _ffn_weights(wg, wu, wd):
    chunked = lambda w: jnp.transpose(w.reshape(D_MODEL, N_FF_CHUNKS, FF_CHUNK), (1, 0, 2)).astype(_BF16)
    return chunked(wg), chunked(wu), wd.astype(_BF16)


def _trunk(x, p):
    batch, seq_len, _ = x.shape
    x2d = x.reshape(batch * seq_len, D_MODEL)
    row = lambda v: v.reshape(1, -1)

    h0 = _proj0(x2d, p["w_in0_a"], p["w_in0_b"], p["ax_cos"][:seq_len], p["ax_sin"][:seq_len],
                p["gqa_gain"], p["ones_bd"], seq_len)
    a_out = _na_attention(h0, p["na_bias"], batch, seq_len).reshape(batch * seq_len, -1)
    b_out = _gqa_attention(h0, batch, seq_len).reshape(batch * seq_len, -1)
    x2d = _out_ln([a_out, b_out], x2d, p["w_out0"], row(p["ln_mix_g"][0]), row(p["ln_mix_b"][0]))
    x2d = _ffn_ln(x2d, *p["ffn0"], row(p["ln_ffn_g"][0]), row(p["ln_ffn_b"][0]))

    h1 = _proj1(x2d, p["w_in1"], p["seq_cos"][:seq_len], p["seq_sin"][:seq_len], seq_len)
    lambda_init = 0.8 - 0.6 * math.exp(-0.3 * 1)
    c_out = _diff_attention(h1, p["lq1"], p["lk1"], p["lq2"], p["lk2"], p["g_subln"],
                            lambda_init, batch, seq_len).reshape(batch * seq_len, -1)
    x2d = _out_ln([c_out], x2d, p["w_out1"], row(p["ln_mix_g"][1]), row(p["ln_mix_b"][1]))
    x2d = _ffn_ln(x2d, *p["ffn1"], row(p["ln_ffn_g"][1]), row(p["ln_ffn_b"][1]))
    return x2d.reshape(batch, seq_len, D_MODEL)


def kernel(x_prompt, x_sample, w_in_mix0, rpb_na, g_q_gqa, g_k_gqa, w_out_mix0, w_in_mix1, lam_q1, lam_k1,
           lam_q2, lam_k2, g_subln, w_out_mix1, ln_mix_g, ln_mix_b, w_ffn_gate, w_ffn_up, w_ffn_down,
           ln_ffn_g, ln_ffn_b):
    max_len = max(x_prompt.shape[1], x_sample.shape[1])
    ax_cos, ax_sin = _axial_tables(max_len)
    seq_cos, seq_sin = _seq_tables(max_len)
    head_of_lane = jnp.arange(LANES) // HEAD_DIM
    w0 = w_in_mix0[0].astype(_BF16)
    na_w = 3 * NA_HEADS * HEAD_DIM
    p = {
        "w_in0_a": w0[:, :na_w],
        "w_in0_b": w0[:, na_w:],
        "ax_cos": ax_cos, "ax_sin": ax_sin, "seq_cos": seq_cos, "seq_sin": seq_sin,
        "gqa_gain": jnp.concatenate([jnp.tile(g_q_gqa[0], GQA_HEADS), jnp.tile(g_k_gqa[0], GQA_KV_HEADS)]
                                    ).reshape(1, -1).astype(_F32),
        "ones_bd": (head_of_lane[:, None] == head_of_lane[None, :]).astype(_BF16),
        "na_bias": _na_bias_table(rpb_na[0]),
        "w_out0": w_out_mix0[0].astype(_BF16),
        "ffn0": _ffn_weights(w_ffn_gate[0], w_ffn_up[0], w_ffn_down[0]),
        "w_in1": w_in_mix1[0].astype(_BF16),
        "lq1": lam_q1[0].reshape(1, -1), "lk1": lam_k1[0].reshape(1, -1),
        "lq2": lam_q2[0].reshape(1, -1), "lk2": lam_k2[0].reshape(1, -1),
        "g_subln": g_subln[0].reshape(1, -1),
        "w_out1": w_out_mix1[0].astype(_BF16),
        "ffn1": _ffn_weights(w_ffn_gate[1], w_ffn_up[1], w_ffn_down[1]),
        "ln_mix_g": ln_mix_g, "ln_mix_b": ln_mix_b, "ln_ffn_g": ln_ffn_g, "ln_ffn_b": ln_ffn_b,
    }
    return _trunk(x_prompt, p), _trunk(x_sample, p)
```

```python
import functools
import math

import jax
import jax.numpy as jnp
from jax import lax
from jax.experimental import pallas as pl
from jax.experimental.pallas import tpu as pltpu

D_MODEL = 1024
DEPTH = 2
GRID_W = 64
HEAD_DIM = 64
NA_HEADS = 8
GQA_HEADS = 8
GQA_KV_HEADS = 2
DIFF_HEADS = 8
NA_WIN_H = 8
NA_WIN_W = 16
ROPE_THETA = 10000.0
D_FF = 2816
LN_EPS = 1e-5
RMS_EPS = 1e-6
SUBLN_EPS = 1e-5
ALPHA = (2.0 * DEPTH) ** 0.25
Q_SCALE = HEAD_DIM ** -0.5
LOG2E = math.log2(math.e)

LANES = 128
FF_CHUNK = 256
N_FF_CHUNKS = D_FF // FF_CHUNK
ROW_TILE = 512
ATT_TQ = 512
ATT_TK = 512
ONES_ROWS = 16
assert ATT_TK == ROW_TILE
MASK_VALUE = -1e30
VMEM_LIMIT = 56 * 1024 * 1024

_BF16 = jnp.bfloat16
_F32 = jnp.float32
_NT = (((1,), (1,)), ((), ()))


def _params(*semantics):
    return pltpu.CompilerParams(dimension_semantics=semantics, vmem_limit_bytes=VMEM_LIMIT)


def _const_spec(shape):
    zeros = (0,) * len(shape)
    return pl.BlockSpec(shape, lambda *_: zeros, pipeline_mode=pl.Buffered(1))


def _layer_norm(xf, gain, bias):
    mu = jnp.mean(xf, axis=-1, keepdims=True)
    xc = xf - mu
    var = jnp.mean(xc * xc, axis=-1, keepdims=True)
    return xc * lax.rsqrt(var + LN_EPS) * gain + bias


def _rotate_pairs(x, half):
    lane = lax.broadcasted_iota(jnp.int32, x.shape, 1)
    first = (lane % (2 * half)) < half
    return jnp.where(first, pltpu.roll(x, LANES - half, 1), pltpu.roll(x, half, 1))


H0_W = 2560


def _proj0_kernel(x_ref, wa_ref, wb_ref, cos_ref, sin_ref, gain_ref, ones_ref, o_ref, vt_ref):
    xb = x_ref[...].astype(_BF16)
    ha = jnp.dot(xb, wa_ref[...], preferred_element_type=_F32)
    o_ref[:, 0:512] = (ha[:, 0:512] * Q_SCALE).astype(_BF16)
    o_ref[:, 512:1536] = ha[:, 512:1536].astype(_BF16)

    hb = jnp.dot(xb, wb_ref[...], preferred_element_type=_F32)
    cos = cos_ref[...]
    sin = sin_ref[...]
    ones_bd = ones_ref[...]
    lo64 = lax.broadcasted_iota(jnp.int32, cos.shape, 1) < HEAD_DIM

    def norm_rope(xg, gain):
        ssq = jnp.dot((xg * xg).astype(_BF16), ones_bd, preferred_element_type=_F32)
        n = xg * lax.rsqrt(ssq * (1.0 / HEAD_DIM) + RMS_EPS) * gain
        return n * cos + _rotate_pairs(n, 16) * sin

    for g in range(4):
        sl = slice(LANES * g, LANES * (g + 1))
        qg = norm_rope(hb[:, sl], gain_ref[:, sl]) * (Q_SCALE * LOG2E)
        o_ref[:, 1536 + LANES * g:1536 + LANES * (g + 1)] = qg.astype(_BF16)

    k = norm_rope(hb[:, 512:640], gain_ref[:, 512:640])
    kr = pltpu.roll(k, HEAD_DIM, 1)
    zero = jnp.zeros_like(k)
    parts = (jnp.where(lo64, k, zero), jnp.where(lo64, zero, kr),
             jnp.where(lo64, kr, zero), jnp.where(lo64, zero, k))
    for i, part in enumerate(parts):
        o_ref[:, 2048 + LANES * i:2048 + LANES * (i + 1)] = part.astype(_BF16)
    vt_ref[...] = hb[:, 640:768].T.astype(_BF16)


def _proj0(x2d, wa, wb, cos, sin, gain, ones_bd, seq_len):
    rows = x2d.shape[0]
    tm = ROW_TILE
    pos_blocks = seq_len // tm
    return pl.pallas_call(
        _proj0_kernel,
        grid=(rows // tm,),
        in_specs=[
            pl.BlockSpec((tm, D_MODEL), lambda i: (i, 0)),
            _const_spec(wa.shape),
            _const_spec(wb.shape),
            pl.BlockSpec((tm, LANES), lambda i: (i % pos_blocks, 0)),
            pl.BlockSpec((tm, LANES), lambda i: (i % pos_blocks, 0)),
            _const_spec(gain.shape),
            _const_spec(ones_bd.shape),
        ],
        out_specs=[pl.BlockSpec((tm, H0_W), lambda i: (i, 0)),
                   pl.BlockSpec((None, LANES, tm), lambda i: (i, 0, 0))],
        out_shape=[jax.ShapeDtypeStruct((rows, H0_W), _BF16),
                   jax.ShapeDtypeStruct((rows // tm, LANES, tm), _BF16)],
        compiler_params=_params("parallel"),
        name="proj0",
    )(x2d, wa, wb, cos, sin, gain, ones_bd)


H1_W = 2 * D_MODEL


def _proj1_kernel(x_ref, w_ref, wvt_ref, cos_ref, sin_ref, o_ref, vt_ref):
    xb = x_ref[...].astype(_BF16)
    cos = cos_ref[...]
    sin = sin_ref[...]
    for part, scale in ((0, Q_SCALE * LOG2E), (1, 1.0)):
        base = part * D_MODEL
        h = jnp.dot(xb, w_ref[:, base:base + D_MODEL], preferred_element_type=_F32)
        for g in range(D_MODEL // LANES):
            xg = h[:, LANES * g:LANES * (g + 1)]
            r = xg * cos + _rotate_pairs(xg, 32) * sin
            if scale != 1.0:
                r = r * scale
            o_ref[:, base + LANES * g:base + LANES * (g + 1)] = r.astype(_BF16)
    vt_ref[...] = lax.dot_general(wvt_ref[...], xb, _NT, preferred_element_type=_F32).astype(_BF16)


def _proj1(x2d, w_qk, w_vt, cos, sin, seq_len):
    rows = x2d.shape[0]
    tm = ROW_TILE
    pos_blocks = seq_len // tm
    return pl.pallas_call(
        _proj1_kernel,
        grid=(rows // tm,),
        in_specs=[
            pl.BlockSpec((tm, D_MODEL), lambda i: (i, 0)),
            _const_spec(w_qk.shape),
            _const_spec(w_vt.shape),
            pl.BlockSpec((tm, LANES), lambda i: (i % pos_blocks, 0)),
            pl.BlockSpec((tm, LANES), lambda i: (i % pos_blocks, 0)),
        ],
        out_specs=[pl.BlockSpec((tm, H1_W), lambda i: (i, 0)),
                   pl.BlockSpec((None, D_MODEL, tm), lambda i: (i, 0, 0))],
        out_shape=[jax.ShapeDtypeStruct((rows, H1_W), _BF16),
                   jax.ShapeDtypeStruct((rows // tm, D_MODEL, tm), _BF16)],
        compiler_params=_params("parallel"),
        name="proj1",
    )(x2d, w_qk, w_vt, cos, sin)


def _with_ones(vt):
    return jnp.concatenate([vt, jnp.ones((ONES_ROWS, vt.shape[1]), vt.dtype)], axis=0)


def _flash_loop(n_k, n_chain, scores_fn, vt_fn, s_ref, acc_ref):
    tq = acc_ref.shape[2]

    def issue_scores(i, slot):
        for c in range(n_chain):
            s_ref[slot, c] = scores_fn(c, i)

    def consume(i, slot, m_state):
        new_state = []
        for c in range(n_chain):
            s = s_ref[slot, c]
            m_prev = m_state[c]
            m_next = jnp.maximum(m_prev, jnp.max(s, axis=0, keepdims=True))
            alpha = jnp.exp2(m_prev - m_next)
            p = jnp.exp2(s - m_next).astype(_BF16)
            pv = jnp.dot(_with_ones(vt_fn(c, i)), p, preferred_element_type=_F32)
            acc_ref[c] = alpha * acc_ref[c] + pv
            new_state.append(m_next)
        return tuple(new_state)

    def two_chunks(j, m_state):
        i = 2 * j
        issue_scores(i + 1, 1)
        m_state = consume(i, 0, m_state)
        issue_scores(i + 2, 0)
        return consume(i + 1, 1, m_state)

    acc_ref[...] = jnp.zeros_like(acc_ref)
    m_state = (jnp.full((1, tq), MASK_VALUE, _F32),) * n_chain
    issue_scores(0, 0)
    m_state = lax.fori_loop(0, n_k // 2 - 1, two_chunks, m_state)
    issue_scores(n_k - 1, 1)
    m_state = consume(n_k - 2, 0, m_state)
    consume(n_k - 1, 1, m_state)


def _gqa_kernel(q_ref, k_ref, vt_ref, o_ref, s_ref, acc_ref, *, seq_len):
    tq, tk = ATT_TQ, ATT_TK
    n_q, n_k = seq_len // tq, seq_len // tk
    d = HEAD_DIM

    def q_body(qi, carry):
        qs = pl.multiple_of(qi * tq, tq)
        for pair in range(GQA_HEADS // 2):
            kvh = pair // (GQA_HEADS // GQA_KV_HEADS // 2)
            q = q_ref[pl.ds(qs, tq), LANES * pair:LANES * (pair + 1)]

            def scores(e, i, q=q, kvh=kvh):
                col = 2 * LANES * kvh + LANES * e
                kk = k_ref[pl.ds(pl.multiple_of(i * tk, tk), tk), col:col + LANES]
                return lax.dot_general(kk, q, _NT, preferred_element_type=_F32)

            def vt(e, i, kvh=kvh):
                return vt_ref[i, d * kvh:d * (kvh + 1), :]

            _flash_loop(n_k, 2, scores, vt, s_ref, acc_ref)
            halves = [acc_ref[e, 0:d, :] * (1.0 / acc_ref[e, d:d + 1, :]) for e in range(2)]
            out = jnp.concatenate(halves, axis=0).T
            o_ref[pl.ds(qs, tq), LANES * pair:LANES * (pair + 1)] = out.astype(_BF16)
        return carry

    lax.fori_loop(0, n_q, q_body, 0)


def _gqa_attention(h0, vt0, batch, seq_len):
    h3 = h0.reshape(batch, seq_len, H0_W)
    width = GQA_HEADS * HEAD_DIM
    return pl.pallas_call(
        functools.partial(_gqa_kernel, seq_len=seq_len),
        grid=(batch,),
        in_specs=[
            pl.BlockSpec((None, seq_len, width), lambda b: (b, 0, 3)),
            pl.BlockSpec((None, seq_len, width), lambda b: (b, 0, 4)),
            pl.BlockSpec((seq_len // ATT_TK, LANES, ATT_TK), lambda b: (b, 0, 0)),
        ],
        out_specs=pl.BlockSpec((None, seq_len, width), lambda b: (b, 0, 0)),
        out_shape=jax.ShapeDtypeStruct((batch, seq_len, width), _BF16),
        scratch_shapes=[pltpu.VMEM((2, 2, ATT_TK, ATT_TQ), _F32),
                        pltpu.VMEM((2, HEAD_DIM + ONES_ROWS, ATT_TQ), _F32)],
        compiler_params=_params("parallel"),
        name="gqa_attention",
    )(h3, h3, vt0)


NA_KEYS = NA_WIN_H * GRID_W


def _na_kernel(q_ref, k_ref, v_ref, bias_ref, o_ref, *, n_rows):
    lo64 = lax.broadcasted_iota(jnp.int32, (NA_KEYS, LANES), 1) < HEAD_DIM
    out_lo64 = lax.broadcasted_iota(jnp.int32, (GRID_W, LANES), 1) < HEAD_DIM

    def row_body(r, carry):
        r0 = jnp.clip(r - NA_WIN_H // 2, 0, n_rows - NA_WIN_H)
        qs = pl.multiple_of(r * GRID_W, GRID_W)
        ks = pl.multiple_of(r0 * GRID_W, GRID_W)
        q = q_ref[pl.ds(qs, GRID_W), :]
        kslab = k_ref[pl.ds(ks, NA_KEYS), :]
        vslab = v_ref[pl.ds(ks, NA_KEYS), :]
        dr0 = r0 - r + (NA_WIN_H - 1)
        zero = jnp.zeros_like(kslab)
        outs = []
        for e in range(2):
            kk = jnp.where(lo64, kslab, zero) if e == 0 else jnp.where(lo64, zero, kslab)
            s = lax.dot_general(q, kk, _NT, preferred_element_type=_F32) + bias_ref[e, dr0]
            m = jnp.max(s, axis=1, keepdims=True)
            p = jnp.exp(s - m)
            l = jnp.sum(p, axis=1, keepdims=True)
            pv = jnp.dot(p.astype(_BF16), vslab, preferred_element_type=_F32)
            outs.append(pv * (1.0 / l))
        o_ref[pl.ds(qs, GRID_W), :] = jnp.where(out_lo64, outs[0], outs[1]).astype(_BF16)
        return carry

    lax.fori_loop(0, n_rows, row_body, 0, unroll=2)


def _na_attention(h0, bias, batch, seq_len):
    h3 = h0.reshape(batch, seq_len, H0_W)
    n_pairs = NA_HEADS // 2
    return pl.pallas_call(
        functools.partial(_na_kernel, n_rows=seq_len // GRID_W),
        grid=(n_pairs, batch),
        in_specs=[
            pl.BlockSpec((None, seq_len, LANES), lambda g, b: (b, 0, g)),
            pl.BlockSpec((None, seq_len, LANES), lambda g, b: (b, 0, n_pairs + g)),
            pl.BlockSpec((None, seq_len, LANES), lambda g, b: (b, 0, 2 * n_pairs + g)),
            pl.BlockSpec((2, NA_WIN_H, GRID_W, NA_KEYS), lambda g, b: (g, 0, 0, 0)),
        ],
        out_specs=pl.BlockSpec((None, seq_len, LANES), lambda g, b: (b, 0, g)),
        out_shape=jax.ShapeDtypeStruct((batch, seq_len, NA_HEADS * HEAD_DIM), _BF16),
        compiler_params=_params("parallel", "parallel"),
        name="na_attention",
    )(h3, h3, h3, bias)


def _na_bias_table(rpb):
    cols = jnp.arange(GRID_W)
    col_start = jnp.clip(cols - NA_WIN_W // 2, 0, GRID_W - NA_WIN_W)
    kc = jnp.arange(GRID_W)
    in_window = (kc[None, :] >= col_start[:, None]) & (kc[None, :] < col_start[:, None] + NA_WIN_W)
    col_idx = jnp.clip(kc[None, :] - cols[:, None] + (NA_WIN_W - 1), 0, 2 * NA_WIN_W - 2)
    per_row = jnp.where(in_window[None, None], rpb[:, :, col_idx], MASK_VALUE)
    dr = jnp.arange(NA_WIN_H)[:, None] + jnp.arange(NA_WIN_H)[None, :]
    table = per_row[:, dr]
    table = jnp.transpose(table, (0, 1, 3, 2, 4))
    return table.reshape(rpb.shape[0], NA_WIN_H, GRID_W, NA_KEYS).astype(_F32)


def _diff_kernel(q_ref, k_ref, vt_ref, lq1_ref, lk1_ref, lq2_ref, lk2_ref, g_ref, o_ref, s_ref, acc_ref,
                 *, seq_len, lambda_init):
    tq, tk = ATT_TQ, ATT_TK
    n_q, n_k = seq_len // tq, seq_len // tk
    d = 2 * HEAD_DIM
    lam = (jnp.exp(jnp.sum(lq1_ref[...] * lk1_ref[...], axis=1, keepdims=True))
           - jnp.exp(jnp.sum(lq2_ref[...] * lk2_ref[...], axis=1, keepdims=True)) + lambda_init)
    lo64 = lax.broadcasted_iota(jnp.int32, (tk, LANES), 1) < HEAD_DIM
    gain = g_ref[...] * (1.0 - lambda_init)

    def q_body(qi, carry):
        qs = pl.multiple_of(qi * tq, tq)
        q = q_ref[pl.ds(qs, tq), :]

        def scores(c, i):
            kblk = k_ref[pl.ds(pl.multiple_of(i * tk, tk), tk), :]
            zero = jnp.zeros_like(kblk)
            kk = jnp.where(lo64, kblk, zero) if c == 0 else jnp.where(lo64, zero, kblk)
            return lax.dot_general(kk, q, _NT, preferred_element_type=_F32)

        def vt(c, i):
            return vt_ref[i]

        _flash_loop(n_k, 2, scores, vt, s_ref, acc_ref)
        o1 = acc_ref[0, 0:d, :] * (1.0 / acc_ref[0, d:d + 1, :])
        o2 = acc_ref[1, 0:d, :] * (1.0 / acc_ref[1, d:d + 1, :])
        o = (o1 - lam * o2).T
        ms = jnp.mean(o * o, axis=1, keepdims=True)
        o_ref[pl.ds(qs, tq), :] = (o * lax.rsqrt(ms + SUBLN_EPS) * gain).astype(_BF16)
        return carry

    lax.fori_loop(0, n_q, q_body, 0)


def _diff_attention(h1, vt1, lq1, lk1, lq2, lk2, g_subln, lambda_init, batch, seq_len):
    h3 = h1.reshape(batch, seq_len, H1_W)
    heads = DIFF_HEADS
    small = lambda n: pl.BlockSpec((1, n), lambda b, h: (0, 0))
    return pl.pallas_call(
        functools.partial(_diff_kernel, seq_len=seq_len, lambda_init=lambda_init),
        grid=(batch, heads),
        in_specs=[
            pl.BlockSpec((None, seq_len, LANES), lambda b, h: (b, 0, h)),
            pl.BlockSpec((None, seq_len, LANES), lambda b, h: (b, 0, heads + h)),
            pl.BlockSpec((seq_len // ATT_TK, LANES, ATT_TK), lambda b, h: (b, h, 0)),
            small(HEAD_DIM), small(HEAD_DIM), small(HEAD_DIM), small(HEAD_DIM), small(2 * HEAD_DIM),
        ],
        out_specs=pl.BlockSpec((None, seq_len, LANES), lambda b, h: (b, 0, h)),
        out_shape=jax.ShapeDtypeStruct((batch, seq_len, D_MODEL), _BF16),
        scratch_shapes=[pltpu.VMEM((2, 2, ATT_TK, ATT_TQ), _F32),
                        pltpu.VMEM((2, 2 * HEAD_DIM + ONES_ROWS, ATT_TQ), _F32)],
        compiler_params=_params("parallel", "parallel"),
        name="diff_attention",
    )(h3, h3, vt1, lq1, lk1, lq2, lk2, g_subln)


def _out_ln_kernel(*refs, n_parts):
    part_refs = refs[:n_parts]
    x_ref, w_ref, g_ref, b_ref, o_ref = refs[n_parts:]
    if n_parts == 1:
        a = part_refs[0][...]
    else:
        a = jnp.concatenate([r[...] for r in part_refs], axis=1)
    mix = jnp.dot(a, w_ref[...], preferred_element_type=_F32)
    o_ref[...] = _layer_norm(ALPHA * x_ref[...] + mix, g_ref[...], b_ref[...])


def _out_ln(parts, x2d, w, gain, bias):
    rows = x2d.shape[0]
    tm = ROW_TILE
    part_specs = [pl.BlockSpec((tm, p.shape[1]), lambda i: (i, 0)) for p in parts]
    return pl.pallas_call(
        functools.partial(_out_ln_kernel, n_parts=len(parts)),
        grid=(rows // tm,),
        in_specs=part_specs + [
            pl.BlockSpec((tm, D_MODEL), lambda i: (i, 0)),
            _const_spec(w.shape), _const_spec(gain.shape), _const_spec(bias.shape),
        ],
        out_specs=pl.BlockSpec((tm, D_MODEL), lambda i: (i, 0)),
        out_shape=jax.ShapeDtypeStruct((rows, D_MODEL), _F32),
        compiler_params=_params("parallel"),
        name="out_proj_ln",
    )(*parts, x2d, w, gain, bias)


def _ffn_ln_kernel(x_ref, wg_ref, wu_ref, wd_ref, g_ref, b_ref, o_ref):
    x = x_ref[...]
    xb = x.astype(_BF16)
    hidden = []
    for c in range(N_FF_CHUNKS):
        gate = jnp.dot(xb, wg_ref[c], preferred_element_type=_F32)
        up = jnp.dot(xb, wu_ref[c], preferred_element_type=_F32)
        hidden.append((gate * jax.nn.sigmoid(gate) * up).astype(_BF16))
    mix = jnp.dot(jnp.concatenate(hidden, axis=1), wd_ref[...], preferred_element_type=_F32)
    o_ref[...] = _layer_norm(ALPHA * x + mix, g_ref[...], b_ref[...])


def _ffn_ln(x2d, wg, wu, wd, gain, bias):
    rows = x2d.shape[0]
    tm = ROW_TILE
    return pl.pallas_call(
        _ffn_ln_kernel,
        grid=(rows // tm,),
        in_specs=[
            pl.BlockSpec((tm, D_MODEL), lambda i: (i, 0)),
            _const_spec(wg.shape), _const_spec(wu.shape), _const_spec(wd.shape),
            _const_spec(gain.shape), _const_spec(bias.shape),
        ],
        out_specs=pl.BlockSpec((tm, D_MODEL), lambda i: (i, 0)),
        out_shape=jax.ShapeDtypeStruct((rows, D_MODEL), _F32),
        compiler_params=_params("parallel"),
        name="swiglu_ln",
    )(x2d, wg, wu, wd, gain, bias)


def _rope_tables(angles):
    cos = jnp.cos(angles)
    sin = jnp.sin(angles)
    return jnp.concatenate([cos, cos], axis=1), jnp.concatenate([-sin, sin], axis=1)


def _axial_tables(seq_len):
    t = jnp.arange(seq_len)
    dim = HEAD_DIM // 2
    inv = ROPE_THETA ** (-jnp.arange(0, dim, 2, dtype=_F32) / dim)
    row_c, row_s = _rope_tables((t // GRID_W).astype(_F32)[:, None] * inv[None, :])
    col_c, col_s = _rope_tables((t % GRID_W).astype(_F32)[:, None] * inv[None, :])
    cos = jnp.concatenate([row_c, col_c] * (LANES // HEAD_DIM), axis=1)
    sin = jnp.concatenate([row_s, col_s] * (LANES // HEAD_DIM), axis=1)
    return cos, sin


def _seq_tables(seq_len):
    t = jnp.arange(seq_len)
    inv = ROPE_THETA ** (-jnp.arange(0, HEAD_DIM, 2, dtype=_F32) / HEAD_DIM)
    cos, sin = _rope_tables(t.astype(_F32)[:, None] * inv[None, :])
    reps = LANES // HEAD_DIM
    return jnp.concatenate([cos] * reps, axis=1), jnp.concatenate([sin] * reps, axis=1)


def _ffn_weights(wg, wu, wd):
    chunked = lambda w: jnp.transpose(w.reshape(D_MODEL, N_FF_CHUNKS, FF_CHUNK), (1, 0, 2)).astype(_BF16)
    return chunked(wg), chunked(wu), wd.astype(_BF16)


def _trunk(x, p):
    batch, seq_len, _ = x.shape
    x2d = x.reshape(batch * seq_len, D_MODEL)
    row = lambda v: v.reshape(1, -1)

    h0, vt0 = _proj0(x2d, p["w_in0_a"], p["w_in0_b"], p["ax_cos"][:seq_len], p["ax_sin"][:seq_len],
                     p["gqa_gain"], p["ones_bd"], seq_len)
    a_out = _na_attention(h0, p["na_bias"], batch, seq_len).reshape(batch * seq_len, -1)
    b_out = _gqa_attention(h0, vt0, batch, seq_len).reshape(batch * seq_len, -1)
    x2d = _out_ln([a_out, b_out], x2d, p["w_out0"], row(p["ln_mix_g"][0]), row(p["ln_mix_b"][0]))
    x2d = _ffn_ln(x2d, *p["ffn0"], row(p["ln_ffn_g"][0]), row(p["ln_ffn_b"][0]))

    h1, vt1 = _proj1(x2d, p["w_in1_qk"], p["w_in1_vt"], p["seq_cos"][:seq_len], p["seq_sin"][:seq_len], seq_len)
    lambda_init = 0.8 - 0.6 * math.exp(-0.3 * 1)
    c_out = _diff_attention(h1, vt1, p["lq1"], p["lk1"], p["lq2"], p["lk2"], p["g_subln"],
                            lambda_init, batch, seq_len).reshape(batch * seq_len, -1)
    x2d = _out_ln([c_out], x2d, p["w_out1"], row(p["ln_mix_g"][1]), row(p["ln_mix_b"][1]))
    x2d = _ffn_ln(x2d, *p["ffn1"], row(p["ln_ffn_g"][1]), row(p["ln_ffn_b"][1]))
    return x2d.reshape(batch, seq_len, D_MODEL)


def kernel(x_prompt, x_sample, w_in_mix0, rpb_na, g_q_gqa, g_k_gqa, w_out_mix0, w_in_mix1, lam_q1, lam_k1,
           lam_q2, lam_k2, g_subln, w_out_mix1, ln_mix_g, ln_mix_b, w_ffn_gate, w_ffn_up, w_ffn_down,
           ln_ffn_g, ln_ffn_b):
    max_len = max(x_prompt.shape[1], x_sample.shape[1])
    ax_cos, ax_sin = _axial_tables(max_len)
    seq_cos, seq_sin = _seq_tables(max_len)
    head_of_lane = jnp.arange(LANES) // HEAD_DIM
    w0 = w_in_mix0[0].astype(_BF16)
    na_w = 3 * NA_HEADS * HEAD_DIM
    p = {
        "w_in0_a": w0[:, :na_w],
        "w_in0_b": w0[:, na_w:],
        "ax_cos": ax_cos, "ax_sin": ax_sin, "seq_cos": seq_cos, "seq_sin": seq_sin,
        "gqa_gain": jnp.concatenate([jnp.tile(g_q_gqa[0], GQA_HEADS), jnp.tile(g_k_gqa[0], GQA_KV_HEADS)]
                                    ).reshape(1, -1).astype(_F32),
        "ones_bd": (head_of_lane[:, None] == head_of_lane[None, :]).astype(_BF16),
        "na_bias": _na_bias_table(rpb_na[0]),
        "w_out0": w_out_mix0[0].astype(_BF16),
        "ffn0": _ffn_weights(w_ffn_gate[0], w_ffn_up[0], w_ffn_down[0]),
        "w_in1_qk": w_in_mix1[0][:, :2 * D_MODEL].astype(_BF16),
        "w_in1_vt": w_in_mix1[0][:, 2 * D_MODEL:].T.astype(_BF16),
        "lq1": lam_q1[0].reshape(1, -1), "lk1": lam_k1[0].reshape(1, -1),
        "lq2": lam_q2[0].reshape(1, -1), "lk2": lam_k2[0].reshape(1, -1),
        "g_subln": g_subln[0].reshape(1, -1),
        "w_out1": w_out_mix1[0].astype(_BF16),
        "ffn1": _ffn_weights(w_ffn_gate[1], w_ffn_up[1], w_ffn_down[1]),
        "ln_mix_g": ln_mix_g, "ln_mix_b": ln_mix_b, "ln_ffn_g": ln_ffn_g, "ln_ffn_b": ln_ffn_b,
    }
    return _trunk(x_prompt, p), _trunk(x_sample, p)
```

```python
import functools
import math

import jax
import jax.numpy as jnp
from jax import lax
from jax.experimental import pallas as pl
from jax.experimental.pallas import tpu as pltpu

D_MODEL = 1024
DEPTH = 2
GRID_W = 64
HEAD_DIM = 64
NA_HEADS = 8
GQA_HEADS = 8
GQA_KV_HEADS = 2
DIFF_HEADS = 8
NA_WIN_H = 8
NA_WIN_W = 16
ROPE_THETA = 10000.0
D_FF = 2816
LN_EPS = 1e-5
RMS_EPS = 1e-6
SUBLN_EPS = 1e-5
ALPHA = (2.0 * DEPTH) ** 0.25
Q_SCALE = HEAD_DIM ** -0.5
LOG2E = math.log2(math.e)

LANES = 128
FF_CHUNK = 256
N_FF_CHUNKS = D_FF // FF_CHUNK
ROW_TILE = 512
ATT_TQ = 512
ATT_TK = 512
ONES_ROWS = 16
assert ATT_TK == ROW_TILE
MASK_VALUE = -1e30
VMEM_LIMIT = 56 * 1024 * 1024

_BF16 = jnp.bfloat16
_F32 = jnp.float32
_NT = (((1,), (1,)), ((), ()))


def _params(*semantics):
    return pltpu.CompilerParams(dimension_semantics=semantics, vmem_limit_bytes=VMEM_LIMIT)


def _const_spec(shape):
    zeros = (0,) * len(shape)
    return pl.BlockSpec(shape, lambda *_: zeros, pipeline_mode=pl.Buffered(1))


def _layer_norm(xf, gain, bias):
    mu = jnp.mean(xf, axis=-1, keepdims=True)
    xc = xf - mu
    var = jnp.mean(xc * xc, axis=-1, keepdims=True)
    return xc * lax.rsqrt(var + LN_EPS) * gain + bias


def _rotate_pairs(x, half):
    lane = lax.broadcasted_iota(jnp.int32, x.shape, 1)
    first = (lane % (2 * half)) < half
    return jnp.where(first, pltpu.roll(x, LANES - half, 1), pltpu.roll(x, half, 1))


H0_W = 2048
NA_VT_CHUNK = 256


def _proj0_kernel(x_ref, wa_ref, wvt_ref, wb_ref, cos_ref, sin_ref, gain_ref, ones_ref, o_ref, na_vt_ref, vt_ref):
    xb = x_ref[...].astype(_BF16)
    ha = jnp.dot(xb, wa_ref[...], preferred_element_type=_F32)
    o_ref[:, 0:512] = (ha[:, 0:512] * (Q_SCALE * LOG2E)).astype(_BF16)
    o_ref[:, 512:1024] = ha[:, 512:1024].astype(_BF16)
    na_vt = lax.dot_general(wvt_ref[...], xb, _NT, preferred_element_type=_F32).astype(_BF16)
    for t in range(na_vt_ref.shape[0]):
        na_vt_ref[t] = na_vt[:, NA_VT_CHUNK * t:NA_VT_CHUNK * (t + 1)]

    hb = jnp.dot(xb, wb_ref[...], preferred_element_type=_F32)
    cos = cos_ref[...]
    sin = sin_ref[...]
    ones_bd = ones_ref[...]
    lo64 = lax.broadcasted_iota(jnp.int32, cos.shape, 1) < HEAD_DIM

    def norm_rope(xg, gain):
        ssq = jnp.dot((xg * xg).astype(_BF16), ones_bd, preferred_element_type=_F32)
        n = xg * lax.rsqrt(ssq * (1.0 / HEAD_DIM) + RMS_EPS) * gain
        return n * cos + _rotate_pairs(n, 16) * sin

    for g in range(4):
        sl = slice(LANES * g, LANES * (g + 1))
        qg = norm_rope(hb[:, sl], gain_ref[:, sl]) * (Q_SCALE * LOG2E)
        o_ref[:, 1024 + LANES * g:1024 + LANES * (g + 1)] = qg.astype(_BF16)

    k = norm_rope(hb[:, 512:640], gain_ref[:, 512:640])
    kr = pltpu.roll(k, HEAD_DIM, 1)
    zero = jnp.zeros_like(k)
    parts = (jnp.where(lo64, k, zero), jnp.where(lo64, zero, kr),
             jnp.where(lo64, kr, zero), jnp.where(lo64, zero, k))
    for i, part in enumerate(parts):
        o_ref[:, 1536 + LANES * i:1536 + LANES * (i + 1)] = part.astype(_BF16)
    vt_ref[...] = hb[:, 640:768].T.astype(_BF16)


def _proj0(x2d, wa, wvt, wb, cos, sin, gain, ones_bd, seq_len):
    rows = x2d.shape[0]
    tm = ROW_TILE
    pos_blocks = seq_len // tm
    na_chunks = tm // NA_VT_CHUNK
    na_w = NA_HEADS * HEAD_DIM
    return pl.pallas_call(
        _proj0_kernel,
        grid=(rows // tm,),
        in_specs=[
            pl.BlockSpec((tm, D_MODEL), lambda i: (i, 0)),
            _const_spec(wa.shape),
            _const_spec(wvt.shape),
            _const_spec(wb.shape),
            pl.BlockSpec((tm, LANES), lambda i: (i % pos_blocks, 0)),
            pl.BlockSpec((tm, LANES), lambda i: (i % pos_blocks, 0)),
            _const_spec(gain.shape),
            _const_spec(ones_bd.shape),
        ],
        out_specs=[pl.BlockSpec((tm, H0_W), lambda i: (i, 0)),
                   pl.BlockSpec((na_chunks, na_w, NA_VT_CHUNK), lambda i: (i, 0, 0)),
                   pl.BlockSpec((None, LANES, tm), lambda i: (i, 0, 0))],
        out_shape=[jax.ShapeDtypeStruct((rows, H0_W), _BF16),
                   jax.ShapeDtypeStruct((rows // NA_VT_CHUNK, na_w, NA_VT_CHUNK), _BF16),
                   jax.ShapeDtypeStruct((rows // tm, LANES, tm), _BF16)],
        compiler_params=_params("parallel"),
        name="proj0",
    )(x2d, wa, wvt, wb, cos, sin, gain, ones_bd)


H1_W = 2 * D_MODEL


def _proj1_kernel(x_ref, w_ref, wvt_ref, cos_ref, sin_ref, o_ref, vt_ref):
    xb = x_ref[...].astype(_BF16)
    cos = cos_ref[...]
    sin = sin_ref[...]
    for part, scale in ((0, Q_SCALE * LOG2E), (1, 1.0)):
        base = part * D_MODEL
        h = jnp.dot(xb, w_ref[:, base:base + D_MODEL], preferred_element_type=_F32)
        for g in range(D_MODEL // LANES):
            xg = h[:, LANES * g:LANES * (g + 1)]
            r = xg * cos + _rotate_pairs(xg, 32) * sin
            if scale != 1.0:
                r = r * scale
            o_ref[:, base + LANES * g:base + LANES * (g + 1)] = r.astype(_BF16)
    vt_ref[...] = lax.dot_general(wvt_ref[...], xb, _NT, preferred_element_type=_F32).astype(_BF16)


def _proj1(x2d, w_qk, w_vt, cos, sin, seq_len):
    rows = x2d.shape[0]
    tm = ROW_TILE
    pos_blocks = seq_len // tm
    return pl.pallas_call(
        _proj1_kernel,
        grid=(rows // tm,),
        in_specs=[
            pl.BlockSpec((tm, D_MODEL), lambda i: (i, 0)),
            _const_spec(w_qk.shape),
            _const_spec(w_vt.shape),
            pl.BlockSpec((tm, LANES), lambda i: (i % pos_blocks, 0)),
            pl.BlockSpec((tm, LANES), lambda i: (i % pos_blocks, 0)),
        ],
        out_specs=[pl.BlockSpec((tm, H1_W), lambda i: (i, 0)),
                   pl.BlockSpec((None, D_MODEL, tm), lambda i: (i, 0, 0))],
        out_shape=[jax.ShapeDtypeStruct((rows, H1_W), _BF16),
                   jax.ShapeDtypeStruct((rows // tm, D_MODEL, tm), _BF16)],
        compiler_params=_params("parallel"),
        name="proj1",
    )(x2d, w_qk, w_vt, cos, sin)


def _with_ones(vt):
    return jnp.concatenate([vt, jnp.ones((ONES_ROWS, vt.shape[1]), vt.dtype)], axis=0)


def _attend_block(n_k, n_chain, scores_fn, next_scores_fn, vt_fn, s_ref, acc_ref):
    m_state = [None] * n_chain
    for i in range(n_k):
        slot = i % 2
        for c in range(n_chain):
            s_ref[1 - slot, c] = scores_fn(c, i + 1) if i + 1 < n_k else next_scores_fn(c)
        for c in range(n_chain):
            s = s_ref[slot, c]
            m_next = jnp.max(s, axis=0, keepdims=True)
            if i > 0:
                m_next = jnp.maximum(m_state[c], m_next)
            p = jnp.exp2(s - m_next).astype(_BF16)
            pv = jnp.dot(_with_ones(vt_fn(c, i)), p, preferred_element_type=_F32)
            acc_ref[c] = pv if i == 0 else jnp.exp2(m_state[c] - m_next) * acc_ref[c] + pv
            m_state[c] = m_next


def _gqa_kernel(q_ref, k_ref, vt_ref, o_ref, s_ref, acc_ref, *, seq_len):
    tq, tk = ATT_TQ, ATT_TK
    n_q, n_k = seq_len // tq, seq_len // tk
    n_pairs = GQA_HEADS // 2
    d = HEAD_DIM

    def scores(qs, pair, e, i):
        kvh = pair // (n_pairs // GQA_KV_HEADS)
        col = 2 * LANES * kvh + LANES * e
        q = q_ref[pl.ds(qs, tq), LANES * pair:LANES * (pair + 1)]
        return lax.dot_general(k_ref[tk * i:tk * (i + 1), col:col + LANES], q, _NT, preferred_element_type=_F32)

    def issue_first(qs, pair):
        for e in range(2):
            s_ref[0, e] = scores(qs, pair, e, 0)

    def q_body(qi, carry):
        qs = pl.multiple_of(qi * tq, tq)
        qs_next = pl.multiple_of(jnp.minimum(qi + 1, n_q - 1) * tq, tq)
        for pair in range(n_pairs):
            kvh = pair // (n_pairs // GQA_KV_HEADS)
            nxt = (qs, pair + 1) if pair + 1 < n_pairs else (qs_next, 0)
            _attend_block(
                n_k, 2,
                lambda e, i, pair=pair: scores(qs, pair, e, i),
                lambda e, nxt=nxt: scores(nxt[0], nxt[1], e, 0),
                lambda e, i, kvh=kvh: vt_ref[i, d * kvh:d * (kvh + 1), :],
                s_ref, acc_ref)
            halves = [acc_ref[e, 0:d, :] * (1.0 / acc_ref[e, d:d + 1, :]) for e in range(2)]
            out = jnp.concatenate(halves, axis=0).T
            o_ref[pl.ds(qs, tq), LANES * pair:LANES * (pair + 1)] = out.astype(_BF16)
        return carry

    issue_first(0, 0)
    lax.fori_loop(0, n_q, q_body, 0)


def _gqa_attention(h0, vt0, batch, seq_len):
    h3 = h0.reshape(batch, seq_len, H0_W)
    width = GQA_HEADS * HEAD_DIM
    return pl.pallas_call(
        functools.partial(_gqa_kernel, seq_len=seq_len),
        grid=(batch,),
        in_specs=[
            pl.BlockSpec((None, seq_len, width), lambda b: (b, 0, 2)),
            pl.BlockSpec((None, seq_len, width), lambda b: (b, 0, 3)),
            pl.BlockSpec((seq_len // ATT_TK, LANES, ATT_TK), lambda b: (b, 0, 0)),
        ],
        out_specs=pl.BlockSpec((None, seq_len, width), lambda b: (b, 0, 0)),
        out_shape=jax.ShapeDtypeStruct((batch, seq_len, width), _BF16),
        scratch_shapes=[pltpu.VMEM((2, 2, ATT_TK, ATT_TQ), _F32),
                        pltpu.VMEM((2, HEAD_DIM + ONES_ROWS, ATT_TQ), _F32)],
        compiler_params=_params("parallel"),
        name="gqa_attention",
    )(h3, h3, vt0)


NA_BLOCK_ROWS = 4
NA_SLAB_ROWS = NA_WIN_H + NA_BLOCK_ROWS
NA_Q = NA_BLOCK_ROWS * GRID_W
NA_KEYS = NA_SLAB_ROWS * GRID_W
assert NA_Q == NA_VT_CHUNK and NA_KEYS % NA_VT_CHUNK == 0


def _na_kernel(q_ref, k_ref, vt_ref, bias_ref, o_ref, s_ref, *, n_rows):
    n_blocks = n_rows // NA_BLOCK_ROWS
    lo64 = lax.broadcasted_iota(jnp.int32, (NA_KEYS, LANES), 1) < HEAD_DIM
    d = HEAD_DIM

    def slab_start(bi):
        return jnp.clip(bi - 1, 0, n_blocks - NA_SLAB_ROWS // NA_BLOCK_ROWS)

    def scores(bi, e):
        q = q_ref[pl.ds(pl.multiple_of(bi * NA_Q, NA_Q), NA_Q), :]
        kslab = k_ref[pl.ds(pl.multiple_of(slab_start(bi) * NA_Q, NA_Q), NA_KEYS), :]
        zero = jnp.zeros_like(kslab)
        kk = jnp.where(lo64, kslab, zero) if e == 0 else jnp.where(lo64, zero, kslab)
        return lax.dot_general(kk, q, _NT, preferred_element_type=_F32)

    def attend(bi, e, slot):
        layout = jnp.where(bi == 0, 0, jnp.where(bi == n_blocks - 1, 2, 1))
        s = s_ref[slot] + bias_ref[e, layout]
        p = jnp.exp2(s - jnp.max(s, axis=0, keepdims=True)).astype(_BF16)
        acc = None
        for t in range(NA_KEYS // NA_VT_CHUNK):
            vt = vt_ref[slab_start(bi) + t, d * e:d * (e + 1), :]
            part = jnp.dot(_with_ones(vt), p[NA_VT_CHUNK * t:NA_VT_CHUNK * (t + 1), :],
                           preferred_element_type=_F32)
            acc = part if acc is None else acc + part
        return acc[0:d, :] * (1.0 / acc[d:d + 1, :])

    def two_blocks(j, carry):
        b0 = 2 * j
        units = [(b0, 0), (b0, 1), (b0 + 1, 0), (b0 + 1, 1), (jnp.minimum(b0 + 2, n_blocks - 1), 0)]
        halves = []
        for u in range(4):
            s_ref[(u + 1) % 2] = scores(*units[u + 1])
            halves.append(attend(*units[u], u % 2))
            if u % 2 == 1:
                qs = pl.multiple_of(units[u][0] * NA_Q, NA_Q)
                o_ref[pl.ds(qs, NA_Q), :] = jnp.concatenate(halves, axis=0).T.astype(_BF16)
                halves = []
        return carry

    s_ref[0] = scores(0, 0)
    lax.fori_loop(0, n_blocks // 2, two_blocks, 0)


def _na_attention(h0, na_vt, bias, batch, seq_len):
    h3 = h0.reshape(batch, seq_len, H0_W)
    n_pairs = NA_HEADS // 2
    chunks = seq_len // NA_VT_CHUNK
    return pl.pallas_call(
        functools.partial(_na_kernel, n_rows=seq_len // GRID_W),
        grid=(n_pairs, batch),
        in_specs=[
            pl.BlockSpec((None, seq_len, LANES), lambda g, b: (b, 0, g)),
            pl.BlockSpec((None, seq_len, LANES), lambda g, b: (b, 0, n_pairs + g)),
            pl.BlockSpec((chunks, LANES, NA_VT_CHUNK), lambda g, b: (b, g, 0)),
            pl.BlockSpec((2, 3, NA_KEYS, NA_Q), lambda g, b: (g, 0, 0, 0)),
        ],
        out_specs=pl.BlockSpec((None, seq_len, LANES), lambda g, b: (b, 0, g)),
        out_shape=jax.ShapeDtypeStruct((batch, seq_len, NA_HEADS * HEAD_DIM), _BF16),
        scratch_shapes=[pltpu.VMEM((2, NA_KEYS, NA_Q), _F32)],
        compiler_params=_params("parallel", "parallel"),
        name="na_attention",
    )(h3, h3, na_vt, bias)


def _na_bias_table(rpb):
    cols = jnp.arange(GRID_W)
    col_start = jnp.clip(cols - NA_WIN_W // 2, 0, GRID_W - NA_WIN_W)
    kc = jnp.arange(GRID_W)
    in_cols = (kc[:, None] >= col_start[None, :]) & (kc[:, None] < col_start[None, :] + NA_WIN_W)
    col_idx = jnp.clip(kc[:, None] - cols[None, :] + (NA_WIN_W - 1), 0, 2 * NA_WIN_W - 2)
    i = jnp.arange(NA_BLOCK_ROWS)
    j = jnp.arange(NA_SLAB_ROWS)
    slab_off = jnp.array([0, -NA_WIN_H // 2, -NA_WIN_H])
    win_off = jnp.stack([jnp.zeros_like(i), i - NA_WIN_H // 2, jnp.full_like(i, -NA_WIN_H // 2)])
    key_row = slab_off[:, None] + j[None, :]
    in_rows = ((key_row[:, :, None] >= win_off[:, None, :])
               & (key_row[:, :, None] < win_off[:, None, :] + NA_WIN_H))
    row_idx = jnp.clip(key_row[:, :, None] - i[None, None, :] + (NA_WIN_H - 1), 0, 2 * NA_WIN_H - 2)
    table = rpb[:, row_idx[:, :, None, :, None], col_idx[None, None, :, None, :]]
    valid = in_rows[:, :, None, :, None] & in_cols[None, None, :, None, :]
    table = jnp.where(valid[None], table * LOG2E, MASK_VALUE)
    return table.reshape(rpb.shape[0], 3, NA_KEYS, NA_Q).astype(_F32)


def _diff_kernel(q_ref, k_ref, vt_ref, lq1_ref, lk1_ref, lq2_ref, lk2_ref, g_ref, o_ref, s_ref, acc_ref,
                 *, seq_len, lambda_init):
    tq, tk = ATT_TQ, ATT_TK
    n_q, n_k = seq_len // tq, seq_len // tk
    d = 2 * HEAD_DIM
    lam = (jnp.exp(jnp.sum(lq1_ref[...] * lk1_ref[...], axis=1, keepdims=True))
           - jnp.exp(jnp.sum(lq2_ref[...] * lk2_ref[...], axis=1, keepdims=True)) + lambda_init)
    lo64 = lax.broadcasted_iota(jnp.int32, (tk, LANES), 1) < HEAD_DIM
    gain = g_ref[...] * (1.0 - lambda_init)

    def scores(qs, c, i):
        kblk = k_ref[tk * i:tk * (i + 1), :]
        zero = jnp.zeros_like(kblk)
        kk = jnp.where(lo64, kblk, zero) if c == 0 else jnp.where(lo64, zero, kblk)
        return lax.dot_general(kk, q_ref[pl.ds(qs, tq), :], _NT, preferred_element_type=_F32)

    def q_body(qi, carry):
        qs = pl.multiple_of(qi * tq, tq)
        qs_next = pl.multiple_of(jnp.minimum(qi + 1, n_q - 1) * tq, tq)
        _attend_block(n_k, 2, lambda c, i: scores(qs, c, i), lambda c: scores(qs_next, c, 0),
                      lambda c, i: vt_ref[i], s_ref, acc_ref)
        o1 = acc_ref[0, 0:d, :] * (1.0 / acc_ref[0, d:d + 1, :])
        o2 = acc_ref[1, 0:d, :] * (1.0 / acc_ref[1, d:d + 1, :])
        o = (o1 - lam * o2).T
        ms = jnp.mean(o * o, axis=1, keepdims=True)
        o_ref[pl.ds(qs, tq), :] = (o * lax.rsqrt(ms + SUBLN_EPS) * gain).astype(_BF16)
        return carry

    for c in range(2):
        s_ref[0, c] = scores(0, c, 0)
    lax.fori_loop(0, n_q, q_body, 0)


def _diff_attention(h1, vt1, lq1, lk1, lq2, lk2, g_subln, lambda_init, batch, seq_len):
    h3 = h1.reshape(batch, seq_len, H1_W)
    heads = DIFF_HEADS
    small = lambda n: pl.BlockSpec((1, n), lambda b, h: (0, 0))
    return pl.pallas_call(
        functools.partial(_diff_kernel, seq_len=seq_len, lambda_init=lambda_init),
        grid=(batch, heads),
        in_specs=[
            pl.BlockSpec((None, seq_len, LANES), lambda b, h: (b, 0, h)),
            pl.BlockSpec((None, seq_len, LANES), lambda b, h: (b, 0, heads + h)),
            pl.BlockSpec((seq_len // ATT_TK, LANES, ATT_TK), lambda b, h: (b, h, 0)),
            small(HEAD_DIM), small(HEAD_DIM), small(HEAD_DIM), small(HEAD_DIM), small(2 * HEAD_DIM),
        ],
        out_specs=pl.BlockSpec((None, seq_len, LANES), lambda b, h: (b, 0, h)),
        out_shape=jax.ShapeDtypeStruct((batch, seq_len, D_MODEL), _BF16),
        scratch_shapes=[pltpu.VMEM((2, 2, ATT_TK, ATT_TQ), _F32),
                        pltpu.VMEM((2, 2 * HEAD_DIM + ONES_ROWS, ATT_TQ), _F32)],
        compiler_params=_params("parallel", "parallel"),
        name="diff_attention",
    )(h3, h3, vt1, lq1, lk1, lq2, lk2, g_subln)


def _out_ln_kernel(*refs, n_parts):
    part_refs = refs[:n_parts]
    x_ref, w_ref, g_ref, b_ref, o_ref = refs[n_parts:]
    if n_parts == 1:
        a = part_refs[0][...]
    else:
        a = jnp.concatenate([r[...] for r in part_refs], axis=1)
    mix = jnp.dot(a, w_ref[...], preferred_element_type=_F32)
    o_ref[...] = _layer_norm(ALPHA * x_ref[...] + mix, g_ref[...], b_ref[...])


def _out_ln(parts, x2d, w, gain, bias):
    rows = x2d.shape[0]
    tm = ROW_TILE
    part_specs = [pl.BlockSpec((tm, p.shape[1]), lambda i: (i, 0)) for p in parts]
    return pl.pallas_call(
        functools.partial(_out_ln_kernel, n_parts=len(parts)),
        grid=(rows // tm,),
        in_specs=part_specs + [
            pl.BlockSpec((tm, D_MODEL), lambda i: (i, 0)),
            _const_spec(w.shape), _const_spec(gain.shape), _const_spec(bias.shape),
        ],
        out_specs=pl.BlockSpec((tm, D_MODEL), lambda i: (i, 0)),
        out_shape=jax.ShapeDtypeStruct((rows, D_MODEL), _F32),
        compiler_params=_params("parallel"),
        name="out_proj_ln",
    )(*parts, x2d, w, gain, bias)


def _ffn_ln_kernel(x_ref, wg_ref, wu_ref, wd_ref, g_ref, b_ref, o_ref):
    x = x_ref[...]
    xb = x.astype(_BF16)
    hidden = []
    for c in range(N_FF_CHUNKS):
        gate = jnp.dot(xb, wg_ref[c], preferred_element_type=_F32)
        up = jnp.dot(xb, wu_ref[c], preferred_element_type=_F32)
        hidden.append((gate * jax.nn.sigmoid(gate) * up).astype(_BF16))
    mix = jnp.dot(jnp.concatenate(hidden, axis=1), wd_ref[...], preferred_element_type=_F32)
    o_ref[...] = _layer_norm(ALPHA * x + mix, g_ref[...], b_ref[...])


def _ffn_ln(x2d, wg, wu, wd, gain, bias):
    rows = x2d.shape[0]
    tm = ROW_TILE
    return pl.pallas_call(
        _ffn_ln_kernel,
        grid=(rows // tm,),
        in_specs=[
            pl.BlockSpec((tm, D_MODEL), lambda i: (i, 0)),
            _const_spec(wg.shape), _const_spec(wu.shape), _const_spec(wd.shape),
            _const_spec(gain.shape), _const_spec(bias.shape),
        ],
        out_specs=pl.BlockSpec((tm, D_MODEL), lambda i: (i, 0)),
        out_shape=jax.ShapeDtypeStruct((rows, D_MODEL), _F32),
        compiler_params=_params("parallel"),
        name="swiglu_ln",
    )(x2d, wg, wu, wd, gain, bias)


def _rope_tables(angles):
    cos = jnp.cos(angles)
    sin = jnp.sin(angles)
    return jnp.concatenate([cos, cos], axis=1), jnp.concatenate([-sin, sin], axis=1)


def _axial_tables(seq_len):
    t = jnp.arange(seq_len)
    dim = HEAD_DIM // 2
    inv = ROPE_THETA ** (-jnp.arange(0, dim, 2, dtype=_F32) / dim)
    row_c, row_s = _rope_tables((t // GRID_W).astype(_F32)[:, None] * inv[None, :])
    col_c, col_s = _rope_tables((t % GRID_W).astype(_F32)[:, None] * inv[None, :])
    cos = jnp.concatenate([row_c, col_c] * (LANES // HEAD_DIM), axis=1)
    sin = jnp.concatenate([row_s, col_s] * (LANES // HEAD_DIM), axis=1)
    return cos, sin


def _seq_tables(seq_len):
    t = jnp.arange(seq_len)
    inv = ROPE_THETA ** (-jnp.arange(0, HEAD_DIM, 2, dtype=_F32) / HEAD_DIM)
    cos, sin = _rope_tables(t.astype(_F32)[:, None] * inv[None, :])
    reps = LANES // HEAD_DIM
    return jnp.concatenate([cos] * reps, axis=1), jnp.concatenate([sin] * reps, axis=1)


def _ffn_weights(wg, wu, wd):
    chunked = lambda w: jnp.transpose(w.reshape(D_MODEL, N_FF_CHUNKS, FF_CHUNK), (1, 0, 2)).astype(_BF16)
    return chunked(wg), chunked(wu), wd.astype(_BF16)


def _trunk(x, p):
    batch, seq_len, _ = x.shape
    x2d = x.reshape(batch * seq_len, D_MODEL)
    row = lambda v: v.reshape(1, -1)

    h0, na_vt, vt0 = _proj0(x2d, p["w_in0_a"], p["w_in0_vt"], p["w_in0_b"], p["ax_cos"][:seq_len],
                            p["ax_sin"][:seq_len], p["gqa_gain"], p["ones_bd"], seq_len)
    a_out = _na_attention(h0, na_vt, p["na_bias"], batch, seq_len).reshape(batch * seq_len, -1)
    b_out = _gqa_attention(h0, vt0, batch, seq_len).reshape(batch * seq_len, -1)
    x2d = _out_ln([a_out, b_out], x2d, p["w_out0"], row(p["ln_mix_g"][0]), row(p["ln_mix_b"][0]))
    x2d = _ffn_ln(x2d, *p["ffn0"], row(p["ln_ffn_g"][0]), row(p["ln_ffn_b"][0]))

    h1, vt1 = _proj1(x2d, p["w_in1_qk"], p["w_in1_vt"], p["seq_cos"][:seq_len], p["seq_sin"][:seq_len], seq_len)
    lambda_init = 0.8 - 0.6 * math.exp(-0.3 * 1)
    c_out = _diff_attention(h1, vt1, p["lq1"], p["lk1"], p["lq2"], p["lk2"], p["g_subln"],
                            lambda_init, batch, seq_len).reshape(batch * seq_len, -1)
    x2d = _out_ln([c_out], x2d, p["w_out1"], row(p["ln_mix_g"][1]), row(p["ln_mix_b"][1]))
    x2d = _ffn_ln(x2d, *p["ffn1"], row(p["ln_ffn_g"][1]), row(p["ln_ffn_b"][1]))
    return x2d.reshape(batch, seq_len, D_MODEL)


def kernel(x_prompt, x_sample, w_in_mix0, rpb_na, g_q_gqa, g_k_gqa, w_out_mix0, w_in_mix1, lam_q1, lam_k1,
           lam_q2, lam_k2, g_subln, w_out_mix1, ln_mix_g, ln_mix_b, w_ffn_gate, w_ffn_up, w_ffn_down,
           ln_ffn_g, ln_ffn_b):
    max_len = max(x_prompt.shape[1], x_sample.shape[1])
    ax_cos, ax_sin = _axial_tables(max_len)
    seq_cos, seq_sin = _seq_tables(max_len)
    head_of_lane = jnp.arange(LANES) // HEAD_DIM
    w0 = w_in_mix0[0].astype(_BF16)
    na_w = NA_HEADS * HEAD_DIM
    p = {
        "w_in0_a": w0[:, :2 * na_w],
        "w_in0_vt": w0[:, 2 * na_w:3 * na_w].T,
        "w_in0_b": w0[:, 3 * na_w:],
        "ax_cos": ax_cos, "ax_sin": ax_sin, "seq_cos": seq_cos, "seq_sin": seq_sin,
        "gqa_gain": jnp.concatenate([jnp.tile(g_q_gqa[0], GQA_HEADS), jnp.tile(g_k_gqa[0], GQA_KV_HEADS)]
                                    ).reshape(1, -1).astype(_F32),
        "ones_bd": (head_of_lane[:, None] == head_of_lane[None, :]).astype(_BF16),
        "na_bias": _na_bias_table(rpb_na[0]),
        "w_out0": w_out_mix0[0].astype(_BF16),
        "ffn0": _ffn_weights(w_ffn_gate[0], w_ffn_up[0], w_ffn_down[0]),
        "w_in1_qk": w_in_mix1[0][:, :2 * D_MODEL].astype(_BF16),
        "w_in1_vt": w_in_mix1[0][:, 2 * D_MODEL:].T.astype(_BF16),
        "lq1": lam_q1[0].reshape(1, -1), "lk1": lam_k1[0].reshape(1, -1),
        "lq2": lam_q2[0].reshape(1, -1), "lk2": lam_k2[0].reshape(1, -1),
        "g_subln": g_subln[0].reshape(1, -1),
        "w_out1": w_out_mix1[0].astype(_BF16),
        "ffn1": _ffn_weights(w_ffn_gate[1], w_ffn_up[1], w_ffn_down[1]),
        "ln_mix_g": ln_mix_g, "ln_mix_b": ln_mix_b, "ln_ffn_g": ln_ffn_g, "ln_ffn_b": ln_ffn_b,
    }
    return _trunk(x_prompt, p), _trunk(x_sample, p)
```

```python
import functools
import math

import jax
import jax.numpy as jnp
import numpy as np
from jax import lax
from jax.experimental import pallas as pl
from jax.experimental.pallas import tpu as pltpu

D_MODEL = 1024
DEPTH = 2
GRID_W = 64
HEAD_DIM = 64
NA_HEADS = 8
GQA_HEADS = 8
GQA_KV_HEADS = 2
DIFF_HEADS = 8
NA_WIN_H = 8
NA_WIN_W = 16
ROPE_THETA = 10000.0
D_FF = 2816
LN_EPS = 1e-5
RMS_EPS = 1e-6
SUBLN_EPS = 1e-5
ALPHA = (2.0 * DEPTH) ** 0.25
Q_SCALE = HEAD_DIM ** -0.5
LOG2E = math.log2(math.e)

LANES = 128
FF_CHUNK = 256
N_FF_CHUNKS = D_FF // FF_CHUNK
ROW_TILE = 512
ATT_TQ = 512
ATT_TK = 512
ONES_ROWS = 16
assert ATT_TK == ROW_TILE
MASK_VALUE = -1e30
VMEM_LIMIT = 56 * 1024 * 1024

_BF16 = jnp.bfloat16
_F32 = jnp.float32
_NT = (((1,), (1,)), ((), ()))


def _params(*semantics):
    return pltpu.CompilerParams(dimension_semantics=semantics, vmem_limit_bytes=VMEM_LIMIT)


def _const_spec(shape):
    zeros = (0,) * len(shape)
    return pl.BlockSpec(shape, lambda *_: zeros, pipeline_mode=pl.Buffered(1))


def _layer_norm(xf, gain, bias):
    mu = jnp.mean(xf, axis=-1, keepdims=True)
    xc = xf - mu
    var = jnp.mean(xc * xc, axis=-1, keepdims=True)
    return xc * lax.rsqrt(var + LN_EPS) * gain + bias


def _rotate_pairs(x, half):
    lane = lax.broadcasted_iota(jnp.int32, x.shape, 1)
    first = (lane % (2 * half)) < half
    return jnp.where(first, pltpu.roll(x, LANES - half, 1), pltpu.roll(x, half, 1))


H0_W = 2048
NA_VT_CHUNK = 256


def _proj0_kernel(x_ref, wa_ref, wvt_ref, wb_ref, cos_ref, sin_ref, gain_ref, ones_ref, o_ref, na_vt_ref, vt_ref):
    xb = x_ref[...].astype(_BF16)
    ha = jnp.dot(xb, wa_ref[...], preferred_element_type=_F32)
    o_ref[:, 0:512] = (ha[:, 0:512] * (Q_SCALE * LOG2E)).astype(_BF16)
    o_ref[:, 512:1024] = ha[:, 512:1024].astype(_BF16)
    na_vt = lax.dot_general(wvt_ref[...], xb, _NT, preferred_element_type=_F32).astype(_BF16)
    for t in range(na_vt_ref.shape[0]):
        na_vt_ref[t] = na_vt[:, NA_VT_CHUNK * t:NA_VT_CHUNK * (t + 1)]

    hb = jnp.dot(xb, wb_ref[...], preferred_element_type=_F32)
    cos = cos_ref[...]
    sin = sin_ref[...]
    ones_bd = ones_ref[...]
    lo64 = lax.broadcasted_iota(jnp.int32, cos.shape, 1) < HEAD_DIM

    def norm_rope(xg, gain):
        ssq = jnp.dot((xg * xg).astype(_BF16), ones_bd, preferred_element_type=_F32)
        n = xg * lax.rsqrt(ssq * (1.0 / HEAD_DIM) + RMS_EPS) * gain
        return n * cos + _rotate_pairs(n, 16) * sin

    for g in range(4):
        sl = slice(LANES * g, LANES * (g + 1))
        qg = norm_rope(hb[:, sl], gain_ref[:, sl]) * (Q_SCALE * LOG2E)
        o_ref[:, 1024 + LANES * g:1024 + LANES * (g + 1)] = qg.astype(_BF16)

    k = norm_rope(hb[:, 512:640], gain_ref[:, 512:640])
    kr = pltpu.roll(k, HEAD_DIM, 1)
    zero = jnp.zeros_like(k)
    parts = (jnp.where(lo64, k, zero), jnp.where(lo64, zero, kr),
             jnp.where(lo64, kr, zero), jnp.where(lo64, zero, k))
    for i, part in enumerate(parts):
        o_ref[:, 1536 + LANES * i:1536 + LANES * (i + 1)] = part.astype(_BF16)
    vt_ref[...] = hb[:, 640:768].T.astype(_BF16)


def _proj0(x2d, wa, wvt, wb, cos, sin, gain, ones_bd, seq_len):
    rows = x2d.shape[0]
    tm = ROW_TILE
    pos_blocks = seq_len // tm
    na_chunks = tm // NA_VT_CHUNK
    na_w = NA_HEADS * HEAD_DIM
    return pl.pallas_call(
        _proj0_kernel,
        grid=(rows // tm,),
        in_specs=[
            pl.BlockSpec((tm, D_MODEL), lambda i: (i, 0)),
            _const_spec(wa.shape),
            _const_spec(wvt.shape),
            _const_spec(wb.shape),
            pl.BlockSpec((tm, LANES), lambda i: (i % pos_blocks, 0)),
            pl.BlockSpec((tm, LANES), lambda i: (i % pos_blocks, 0)),
            _const_spec(gain.shape),
            _const_spec(ones_bd.shape),
        ],
        out_specs=[pl.BlockSpec((tm, H0_W), lambda i: (i, 0)),
                   pl.BlockSpec((na_chunks, na_w, NA_VT_CHUNK), lambda i: (i, 0, 0)),
                   pl.BlockSpec((None, LANES, tm), lambda i: (i, 0, 0))],
        out_shape=[jax.ShapeDtypeStruct((rows, H0_W), _BF16),
                   jax.ShapeDtypeStruct((rows // NA_VT_CHUNK, na_w, NA_VT_CHUNK), _BF16),
                   jax.ShapeDtypeStruct((rows // tm, LANES, tm), _BF16)],
        compiler_params=_params("parallel"),
        name="proj0",
    )(x2d, wa, wvt, wb, cos, sin, gain, ones_bd)


H1_W = 2 * D_MODEL


def _proj1_kernel(x_ref, w_ref, wvt_ref, cos_ref, sin_ref, o_ref, vt_ref):
    xb = x_ref[...].astype(_BF16)
    cos = cos_ref[...]
    sin = sin_ref[...]
    for part, scale in ((0, Q_SCALE * LOG2E), (1, 1.0)):
        base = part * D_MODEL
        h = jnp.dot(xb, w_ref[:, base:base + D_MODEL], preferred_element_type=_F32)
        for g in range(D_MODEL // LANES):
            xg = h[:, LANES * g:LANES * (g + 1)]
            r = xg * cos + _rotate_pairs(xg, 32) * sin
            if scale != 1.0:
                r = r * scale
            o_ref[:, base + LANES * g:base + LANES * (g + 1)] = r.astype(_BF16)
    vt_ref[...] = lax.dot_general(wvt_ref[...], xb, _NT, preferred_element_type=_F32).astype(_BF16)


def _proj1(x2d, w_qk, w_vt, cos, sin, seq_len):
    rows = x2d.shape[0]
    tm = ROW_TILE
    pos_blocks = seq_len // tm
    return pl.pallas_call(
        _proj1_kernel,
        grid=(rows // tm,),
        in_specs=[
            pl.BlockSpec((tm, D_MODEL), lambda i: (i, 0)),
            _const_spec(w_qk.shape),
            _const_spec(w_vt.shape),
            pl.BlockSpec((tm, LANES), lambda i: (i % pos_blocks, 0)),
            pl.BlockSpec((tm, LANES), lambda i: (i % pos_blocks, 0)),
        ],
        out_specs=[pl.BlockSpec((tm, H1_W), lambda i: (i, 0)),
                   pl.BlockSpec((None, D_MODEL, tm), lambda i: (i, 0, 0))],
        out_shape=[jax.ShapeDtypeStruct((rows, H1_W), _BF16),
                   jax.ShapeDtypeStruct((rows // tm, D_MODEL, tm), _BF16)],
        compiler_params=_params("parallel"),
        name="proj1",
    )(x2d, w_qk, w_vt, cos, sin)


def _with_ones(vt):
    return jnp.concatenate([vt, jnp.ones((ONES_ROWS, vt.shape[1]), vt.dtype)], axis=0)


def _attend_block(n_k, n_chain, scores_fn, next_scores_fn, vt_fn, s_ref, acc_ref):
    m_state = [None] * n_chain
    for i in range(n_k):
        slot = i % 2
        for c in range(n_chain):
            s_ref[1 - slot, c] = scores_fn(c, i + 1) if i + 1 < n_k else next_scores_fn(c)
        for c in range(n_chain):
            s = s_ref[slot, c]
            m_next = jnp.max(s, axis=0, keepdims=True)
            if i > 0:
                m_next = jnp.maximum(m_state[c], m_next)
            p = jnp.exp2(s - m_next).astype(_BF16)
            pv = jnp.dot(_with_ones(vt_fn(c, i)), p, preferred_element_type=_F32)
            acc_ref[c] = pv if i == 0 else jnp.exp2(m_state[c] - m_next) * acc_ref[c] + pv
            m_state[c] = m_next


def _gqa_kernel(q_ref, k_ref, vt_ref, o_ref, s_ref, acc_ref, *, seq_len):
    tq, tk = ATT_TQ, ATT_TK
    n_q, n_k = seq_len // tq, seq_len // tk
    n_pairs = GQA_HEADS // 2
    d = HEAD_DIM

    def scores(qs, pair, e, i):
        kvh = pair // (n_pairs // GQA_KV_HEADS)
        col = 2 * LANES * kvh + LANES * e
        q = q_ref[pl.ds(qs, tq), LANES * pair:LANES * (pair + 1)]
        return lax.dot_general(k_ref[tk * i:tk * (i + 1), col:col + LANES], q, _NT, preferred_element_type=_F32)

    def issue_first(qs, pair):
        for e in range(2):
            s_ref[0, e] = scores(qs, pair, e, 0)

    def q_body(qi, carry):
        qs = pl.multiple_of(qi * tq, tq)
        qs_next = pl.multiple_of(jnp.minimum(qi + 1, n_q - 1) * tq, tq)
        for pair in range(n_pairs):
            kvh = pair // (n_pairs // GQA_KV_HEADS)
            nxt = (qs, pair + 1) if pair + 1 < n_pairs else (qs_next, 0)
            _attend_block(
                n_k, 2,
                lambda e, i, pair=pair: scores(qs, pair, e, i),
                lambda e, nxt=nxt: scores(nxt[0], nxt[1], e, 0),
                lambda e, i, kvh=kvh: vt_ref[i, d * kvh:d * (kvh + 1), :],
                s_ref, acc_ref)
            halves = [acc_ref[e, 0:d, :] * (1.0 / acc_ref[e, d:d + 1, :]) for e in range(2)]
            out = jnp.concatenate(halves, axis=0).T
            o_ref[pl.ds(qs, tq), LANES * pair:LANES * (pair + 1)] = out.astype(_BF16)
        return carry

    issue_first(0, 0)
    lax.fori_loop(0, n_q, q_body, 0)


def _gqa_attention(h0, vt0, batch, seq_len):
    h3 = h0.reshape(batch, seq_len, H0_W)
    width = GQA_HEADS * HEAD_DIM
    return pl.pallas_call(
        functools.partial(_gqa_kernel, seq_len=seq_len),
        grid=(batch,),
        in_specs=[
            pl.BlockSpec((None, seq_len, width), lambda b: (b, 0, 2)),
            pl.BlockSpec((None, seq_len, width), lambda b: (b, 0, 3)),
            pl.BlockSpec((seq_len // ATT_TK, LANES, ATT_TK), lambda b: (b, 0, 0)),
        ],
        out_specs=pl.BlockSpec((None, seq_len, width), lambda b: (b, 0, 0)),
        out_shape=jax.ShapeDtypeStruct((batch, seq_len, width), _BF16),
        scratch_shapes=[pltpu.VMEM((2, 2, ATT_TK, ATT_TQ), _F32),
                        pltpu.VMEM((2, HEAD_DIM + ONES_ROWS, ATT_TQ), _F32)],
        compiler_params=_params("parallel"),
        name="gqa_attention",
    )(h3, h3, vt0)


NA_BLOCK_ROWS = 4
NA_SLAB_ROWS = NA_WIN_H + NA_BLOCK_ROWS
NA_Q = NA_BLOCK_ROWS * GRID_W
NA_KEYS = NA_SLAB_ROWS * GRID_W
assert NA_Q == NA_VT_CHUNK and NA_KEYS % NA_VT_CHUNK == 0


def _na_kernel(q_ref, k_ref, vt_ref, bias_ref, o_ref, s_ref, *, n_rows):
    n_blocks = n_rows // NA_BLOCK_ROWS
    lo64 = lax.broadcasted_iota(jnp.int32, (NA_KEYS, LANES), 1) < HEAD_DIM
    d = HEAD_DIM

    def slab_start(bi):
        return jnp.clip(bi - 1, 0, n_blocks - NA_SLAB_ROWS // NA_BLOCK_ROWS)

    def scores(bi, e):
        q = q_ref[pl.ds(pl.multiple_of(bi * NA_Q, NA_Q), NA_Q), :]
        kslab = k_ref[pl.ds(pl.multiple_of(slab_start(bi) * NA_Q, NA_Q), NA_KEYS), :]
        zero = jnp.zeros_like(kslab)
        kk = jnp.where(lo64, kslab, zero) if e == 0 else jnp.where(lo64, zero, kslab)
        return lax.dot_general(kk, q, _NT, preferred_element_type=_F32)

    def attend(bi, e, slot):
        layout = jnp.where(bi == 0, 0, jnp.where(bi == n_blocks - 1, 2, 1))
        s = s_ref[slot] + bias_ref[e, layout]
        p = jnp.exp2(s - jnp.max(s, axis=0, keepdims=True)).astype(_BF16)
        acc = None
        for t in range(NA_KEYS // NA_VT_CHUNK):
            vt = vt_ref[slab_start(bi) + t, d * e:d * (e + 1), :]
            part = jnp.dot(_with_ones(vt), p[NA_VT_CHUNK * t:NA_VT_CHUNK * (t + 1), :],
                           preferred_element_type=_F32)
            acc = part if acc is None else acc + part
        return acc[0:d, :] * (1.0 / acc[d:d + 1, :])

    def two_blocks(j, carry):
        b0 = 2 * j
        units = [(b0, 0), (b0, 1), (b0 + 1, 0), (b0 + 1, 1), (jnp.minimum(b0 + 2, n_blocks - 1), 0)]
        halves = []
        for u in range(4):
            s_ref[(u + 1) % 2] = scores(*units[u + 1])
            halves.append(attend(*units[u], u % 2))
            if u % 2 == 1:
                qs = pl.multiple_of(units[u][0] * NA_Q, NA_Q)
                o_ref[pl.ds(qs, NA_Q), :] = jnp.concatenate(halves, axis=0).T.astype(_BF16)
                halves = []
        return carry

    s_ref[0] = scores(0, 0)
    lax.fori_loop(0, n_blocks // 2, two_blocks, 0)


def _na_attention(h0, na_vt, bias, batch, seq_len):
    h3 = h0.reshape(batch, seq_len, H0_W)
    n_pairs = NA_HEADS // 2
    chunks = seq_len // NA_VT_CHUNK
    return pl.pallas_call(
        functools.partial(_na_kernel, n_rows=seq_len // GRID_W),
        grid=(n_pairs, batch),
        in_specs=[
            pl.BlockSpec((None, seq_len, LANES), lambda g, b: (b, 0, g)),
            pl.BlockSpec((None, seq_len, LANES), lambda g, b: (b, 0, n_pairs + g)),
            pl.BlockSpec((chunks, LANES, NA_VT_CHUNK), lambda g, b: (b, g, 0)),
            pl.BlockSpec((2, 3, NA_KEYS, NA_Q), lambda g, b: (g, 0, 0, 0)),
        ],
        out_specs=pl.BlockSpec((None, seq_len, LANES), lambda g, b: (b, 0, g)),
        out_shape=jax.ShapeDtypeStruct((batch, seq_len, NA_HEADS * HEAD_DIM), _BF16),
        scratch_shapes=[pltpu.VMEM((2, NA_KEYS, NA_Q), _F32)],
        compiler_params=_params("parallel", "parallel"),
        name="na_attention",
    )(h3, h3, na_vt, bias)


def _na_bias_table(rpb):
    cols = np.arange(GRID_W)
    col_start = np.clip(cols - NA_WIN_W // 2, 0, GRID_W - NA_WIN_W)
    kc = np.arange(GRID_W)
    in_cols = (kc[:, None] >= col_start[None, :]) & (kc[:, None] < col_start[None, :] + NA_WIN_W)
    col_idx = kc[:, None] - cols[None, :] + (NA_WIN_W - 1)
    i = np.arange(NA_BLOCK_ROWS)
    j = np.arange(NA_SLAB_ROWS)
    slab_off = np.array([0, -NA_WIN_H // 2, -NA_WIN_H])
    win_off = np.stack([np.zeros_like(i), i - NA_WIN_H // 2, np.full_like(i, -NA_WIN_H // 2)])
    key_row = slab_off[:, None] + j[None, :]
    in_rows = ((key_row[:, :, None] >= win_off[:, None, :])
               & (key_row[:, :, None] < win_off[:, None, :] + NA_WIN_H))
    row_idx = key_row[:, :, None] - i[None, None, :] + (NA_WIN_H - 1)
    row_pick = (in_rows[..., None] & (row_idx[..., None] == np.arange(2 * NA_WIN_H - 1))).astype(np.float32)
    col_pick = (in_cols[..., None] & (col_idx[..., None] == np.arange(2 * NA_WIN_W - 1))).astype(np.float32)
    by_col = jnp.einsum("hab,kcb->hakc", rpb, col_pick, precision=lax.Precision.HIGHEST)
    table = jnp.einsum("ljia,hakc->hljkic", row_pick, by_col, precision=lax.Precision.HIGHEST)
    valid = in_rows[:, :, None, :, None] & in_cols[None, None, :, None, :]
    table = jnp.where(valid[None], table * LOG2E, MASK_VALUE)
    return table.reshape(rpb.shape[0], 3, NA_KEYS, NA_Q).astype(_F32)


def _diff_kernel(q_ref, k_ref, vt_ref, lq1_ref, lk1_ref, lq2_ref, lk2_ref, g_ref, o_ref, s_ref, acc_ref,
                 *, seq_len, lambda_init):
    tq, tk = ATT_TQ, ATT_TK
    n_q, n_k = seq_len // tq, seq_len // tk
    d = 2 * HEAD_DIM
    lam = (jnp.exp(jnp.sum(lq1_ref[...] * lk1_ref[...], axis=1, keepdims=True))
           - jnp.exp(jnp.sum(lq2_ref[...] * lk2_ref[...], axis=1, keepdims=True)) + lambda_init)
    lo64 = lax.broadcasted_iota(jnp.int32, (tk, LANES), 1) < HEAD_DIM
    gain = g_ref[...] * (1.0 - lambda_init)

    def scores(qs, c, i):
        kblk = k_ref[tk * i:tk * (i + 1), :]
        zero = jnp.zeros_like(kblk)
        kk = jnp.where(lo64, kblk, zero) if c == 0 else jnp.where(lo64, zero, kblk)
        return lax.dot_general(kk, q_ref[pl.ds(qs, tq), :], _NT, preferred_element_type=_F32)

    def q_body(qi, carry):
        qs = pl.multiple_of(qi * tq, tq)
        qs_next = pl.multiple_of(jnp.minimum(qi + 1, n_q - 1) * tq, tq)
        _attend_block(n_k, 2, lambda c, i: scores(qs, c, i), lambda c: scores(qs_next, c, 0),
                      lambda c, i: vt_ref[i], s_ref, acc_ref)
        o1 = acc_ref[0, 0:d, :] * (1.0 / acc_ref[0, d:d + 1, :])
        o2 = acc_ref[1, 0:d, :] * (1.0 / acc_ref[1, d:d + 1, :])
        o = (o1 - lam * o2).T
        ms = jnp.mean(o * o, axis=1, keepdims=True)
        o_ref[pl.ds(qs, tq), :] = (o * lax.rsqrt(ms + SUBLN_EPS) * gain).astype(_BF16)
        return carry

    for c in range(2):
        s_ref[0, c] = scores(0, c, 0)
    lax.fori_loop(0, n_q, q_body, 0, unroll=2)


def _diff_attention(h1, vt1, lq1, lk1, lq2, lk2, g_subln, lambda_init, batch, seq_len):
    h3 = h1.reshape(batch, seq_len, H1_W)
    heads = DIFF_HEADS
    small = lambda n: pl.BlockSpec((1, n), lambda b, h: (0, 0))
    return pl.pallas_call(
        functools.partial(_diff_kernel, seq_len=seq_len, lambda_init=lambda_init),
        grid=(batch, heads),
        in_specs=[
            pl.BlockSpec((None, seq_len, LANES), lambda b, h: (b, 0, h)),
            pl.BlockSpec((None, seq_len, LANES), lambda b, h: (b, 0, heads + h)),
            pl.BlockSpec((seq_len // ATT_TK, LANES, ATT_TK), lambda b, h: (b, h, 0)),
            small(HEAD_DIM), small(HEAD_DIM), small(HEAD_DIM), small(HEAD_DIM), small(2 * HEAD_DIM),
        ],
        out_specs=pl.BlockSpec((None, seq_len, LANES), lambda b, h: (b, 0, h)),
        out_shape=jax.ShapeDtypeStruct((batch, seq_len, D_MODEL), _BF16),
        scratch_shapes=[pltpu.VMEM((2, 2, ATT_TK, ATT_TQ), _F32),
                        pltpu.VMEM((2, 2 * HEAD_DIM + ONES_ROWS, ATT_TQ), _F32)],
        compiler_params=_params("parallel", "parallel"),
        name="diff_attention",
    )(h3, h3, vt1, lq1, lk1, lq2, lk2, g_subln)


def _layer_tail_kernel(*refs, n_parts):
    part_refs = refs[:n_parts]
    (x_ref, wo_ref, g1_ref, b1_ref, wg_ref, wu_ref, wd_ref, g2_ref, b2_ref, o_ref) = refs[n_parts:]
    if n_parts == 1:
        a = part_refs[0][...]
    else:
        a = jnp.concatenate([r[...] for r in part_refs], axis=1)
    mix = jnp.dot(a, wo_ref[...], preferred_element_type=_F32)
    x1 = _layer_norm(ALPHA * x_ref[...] + mix, g1_ref[...], b1_ref[...])
    xb = x1.astype(_BF16)
    hidden = []
    for c in range(N_FF_CHUNKS):
        gate = jnp.dot(xb, wg_ref[c], preferred_element_type=_F32)
        up = jnp.dot(xb, wu_ref[c], preferred_element_type=_F32)
        hidden.append((gate * jax.nn.sigmoid(gate) * up).astype(_BF16))
    ffn = jnp.dot(jnp.concatenate(hidden, axis=1), wd_ref[...], preferred_element_type=_F32)
    o_ref[...] = _layer_norm(ALPHA * x1 + ffn, g2_ref[...], b2_ref[...])


def _layer_tail(parts, x2d, w_out, g1, b1, wg, wu, wd, g2, b2):
    rows = x2d.shape[0]
    tm = ROW_TILE
    part_specs = [pl.BlockSpec((tm, p.shape[1]), lambda i: (i, 0)) for p in parts]
    consts = (w_out, g1, b1, wg, wu, wd, g2, b2)
    return pl.pallas_call(
        functools.partial(_layer_tail_kernel, n_parts=len(parts)),
        grid=(rows // tm,),
        in_specs=part_specs + [pl.BlockSpec((tm, D_MODEL), lambda i: (i, 0))] + [_const_spec(c.shape) for c in consts],
        out_specs=pl.BlockSpec((tm, D_MODEL), lambda i: (i, 0)),
        out_shape=jax.ShapeDtypeStruct((rows, D_MODEL), _F32),
        compiler_params=_params("parallel"),
        name="layer_tail",
    )(*parts, x2d, *consts)


def _rope_tables(angles):
    cos = jnp.cos(angles)
    sin = jnp.sin(angles)
    return jnp.concatenate([cos, cos], axis=1), jnp.concatenate([-sin, sin], axis=1)


def _axial_tables(seq_len):
    t = jnp.arange(seq_len)
    dim = HEAD_DIM // 2
    inv = ROPE_THETA ** (-jnp.arange(0, dim, 2, dtype=_F32) / dim)
    row_c, row_s = _rope_tables((t // GRID_W).astype(_F32)[:, None] * inv[None, :])
    col_c, col_s = _rope_tables((t % GRID_W).astype(_F32)[:, None] * inv[None, :])
    cos = jnp.concatenate([row_c, col_c] * (LANES // HEAD_DIM), axis=1)
    sin = jnp.concatenate([row_s, col_s] * (LANES // HEAD_DIM), axis=1)
    return cos, sin


def _seq_tables(seq_len):
    t = jnp.arange(seq_len)
    inv = ROPE_THETA ** (-jnp.arange(0, HEAD_DIM, 2, dtype=_F32) / HEAD_DIM)
    cos, sin = _rope_tables(t.astype(_F32)[:, None] * inv[None, :])
    reps = LANES // HEAD_DIM
    return jnp.concatenate([cos] * reps, axis=1), jnp.concatenate([sin] * reps, axis=1)


def _ffn_weights(wg, wu, wd):
    chunked = lambda w: jnp.transpose(w.reshape(D_MODEL, N_FF_CHUNKS, FF_CHUNK), (1, 0, 2)).astype(_BF16)
    return chunked(wg), chunked(wu), wd.astype(_BF16)


def _trunk(x, p):
    batch, seq_len, _ = x.shape
    x2d = x.reshape(batch * seq_len, D_MODEL)
    row = lambda v: v.reshape(1, -1)

    h0, na_vt, vt0 = _proj0(x2d, p["w_in0_a"], p["w_in0_vt"], p["w_in0_b"], p["ax_cos"][:seq_len],
                            p["ax_sin"][:seq_len], p["gqa_gain"], p["ones_bd"], seq_len)
    a_out = _na_attention(h0, na_vt, p["na_bias"], batch, seq_len).reshape(batch * seq_len, -1)
    b_out = _gqa_attention(h0, vt0, batch, seq_len).reshape(batch * seq_len, -1)
    x2d = _layer_tail([a_out, b_out], x2d, p["w_out0"], row(p["ln_mix_g"][0]), row(p["ln_mix_b"][0]),
                      *p["ffn0"], row(p["ln_ffn_g"][0]), row(p["ln_ffn_b"][0]))

    h1, vt1 = _proj1(x2d, p["w_in1_qk"], p["w_in1_vt"], p["seq_cos"][:seq_len], p["seq_sin"][:seq_len], seq_len)
    lambda_init = 0.8 - 0.6 * math.exp(-0.3 * 1)
    c_out = _diff_attention(h1, vt1, p["lq1"], p["lk1"], p["lq2"], p["lk2"], p["g_subln"],
                            lambda_init, batch, seq_len).reshape(batch * seq_len, -1)
    x2d = _layer_tail([c_out], x2d, p["w_out1"], row(p["ln_mix_g"][1]), row(p["ln_mix_b"][1]),
                      *p["ffn1"], row(p["ln_ffn_g"][1]), row(p["ln_ffn_b"][1]))
    return x2d.reshape(batch, seq_len, D_MODEL)


def kernel(x_prompt, x_sample, w_in_mix0, rpb_na, g_q_gqa, g_k_gqa, w_out_mix0, w_in_mix1, lam_q1, lam_k1,
           lam_q2, lam_k2, g_subln, w_out_mix1, ln_mix_g, ln_mix_b, w_ffn_gate, w_ffn_up, w_ffn_down,
           ln_ffn_g, ln_ffn_b):
    max_len = max(x_prompt.shape[1], x_sample.shape[1])
    ax_cos, ax_sin = _axial_tables(max_len)
    seq_cos, seq_sin = _seq_tables(max_len)
    head_of_lane = jnp.arange(LANES) // HEAD_DIM
    w0 = w_in_mix0[0].astype(_BF16)
    na_w = NA_HEADS * HEAD_DIM
    p = {
        "w_in0_a": w0[:, :2 * na_w],
        "w_in0_vt": w0[:, 2 * na_w:3 * na_w].T,
        "w_in0_b": w0[:, 3 * na_w:],
        "ax_cos": ax_cos, "ax_sin": ax_sin, "seq_cos": seq_cos, "seq_sin": seq_sin,
        "gqa_gain": jnp.concatenate([jnp.tile(g_q_gqa[0], GQA_HEADS), jnp.tile(g_k_gqa[0], GQA_KV_HEADS)]
                                    ).reshape(1, -1).astype(_F32),
        "ones_bd": (head_of_lane[:, None] == head_of_lane[None, :]).astype(_BF16),
        "na_bias": _na_bias_table(rpb_na[0]),
        "w_out0": w_out_mix0[0].astype(_BF16),
        "ffn0": _ffn_weights(w_ffn_gate[0], w_ffn_up[0], w_ffn_down[0]),
        "w_in1_qk": w_in_mix1[0][:, :2 * D_MODEL].astype(_BF16),
        "w_in1_vt": w_in_mix1[0][:, 2 * D_MODEL:].T.astype(_BF16),
        "lq1": lam_q1[0].reshape(1, -1), "lk1": lam_k1[0].reshape(1, -1),
        "lq2": lam_q2[0].reshape(1, -1), "lk2": lam_k2[0].reshape(1, -1),
        "g_subln": g_subln[0].reshape(1, -1),
        "w_out1": w_out_mix1[0].astype(_BF16),
        "ffn1": _ffn_weights(w_ffn_gate[1], w_ffn_up[1], w_ffn_down[1]),
        "ln_mix_g": ln_mix_g, "ln_mix_b": ln_mix_b, "ln_ffn_g": ln_ffn_g, "ln_ffn_b": ln_ffn_b,
    }
    return _trunk(x_prompt, p), _trunk(x_sample, p)
```

```python
import functools
import math

import jax
import jax.numpy as jnp
import numpy as np
from jax import lax
from jax.experimental import pallas as pl
from jax.experimental.pallas import tpu as pltpu

D_MODEL = 1024
DEPTH = 2
GRID_W = 64
HEAD_DIM = 64
NA_HEADS = 8
GQA_HEADS = 8
GQA_KV_HEADS = 2
DIFF_HEADS = 8
NA_WIN_H = 8
NA_WIN_W = 16
ROPE_THETA = 10000.0
D_FF = 2816
LN_EPS = 1e-5
RMS_EPS = 1e-6
SUBLN_EPS = 1e-5
ALPHA = (2.0 * DEPTH) ** 0.25
Q_SCALE = HEAD_DIM ** -0.5
LOG2E = math.log2(math.e)

LANES = 128
FF_CHUNK = 256
N_FF_CHUNKS = D_FF // FF_CHUNK
ROW_TILE = 512
ATT_TQ = 512
ATT_TK = 512
ONES_ROWS = 16
assert ATT_TK == ROW_TILE
MASK_VALUE = -1e30
VMEM_LIMIT = 56 * 1024 * 1024

_BF16 = jnp.bfloat16
_F32 = jnp.float32
_NT = (((1,), (1,)), ((), ()))


def _params(*semantics):
    return pltpu.CompilerParams(dimension_semantics=semantics, vmem_limit_bytes=VMEM_LIMIT)


def _const_spec(shape):
    zeros = (0,) * len(shape)
    return pl.BlockSpec(shape, lambda *_: zeros, pipeline_mode=pl.Buffered(1))


def _layer_norm(xf, gain, bias):
    mu = jnp.mean(xf, axis=-1, keepdims=True)
    xc = xf - mu
    var = jnp.mean(xc * xc, axis=-1, keepdims=True)
    return xc * lax.rsqrt(var + LN_EPS) * gain + bias


def _rotate_pairs(x, half):
    lane = lax.broadcasted_iota(jnp.int32, x.shape, 1)
    first = (lane % (2 * half)) < half
    return jnp.where(first, pltpu.roll(x, LANES - half, 1), pltpu.roll(x, half, 1))


H0_W = 2048
NA_VT_CHUNK = 256


def _proj0_kernel(x_ref, wa_ref, wvt_ref, wb_ref, cos_ref, sin_ref, gain_ref, ones_ref, o_ref, na_vt_ref, vt_ref):
    xb = x_ref[...].astype(_BF16)
    hb = jnp.dot(xb, wb_ref[...], preferred_element_type=_F32)
    cos = cos_ref[...]
    sin = sin_ref[...]
    ones_bd = ones_ref[...]
    lo64 = lax.broadcasted_iota(jnp.int32, cos.shape, 1) < HEAD_DIM

    def norm_rope(xg, gain):
        ssq = jnp.dot((xg * xg).astype(_BF16), ones_bd, preferred_element_type=_F32)
        n = xg * lax.rsqrt(ssq * (1.0 / HEAD_DIM) + RMS_EPS) * gain
        return n * cos + _rotate_pairs(n, 16) * sin

    for g in range(4):
        sl = slice(LANES * g, LANES * (g + 1))
        qg = norm_rope(hb[:, sl], gain_ref[:, sl]) * (Q_SCALE * LOG2E)
        o_ref[:, 1024 + LANES * g:1024 + LANES * (g + 1)] = qg.astype(_BF16)

    k = norm_rope(hb[:, 512:640], gain_ref[:, 512:640])
    kr = pltpu.roll(k, HEAD_DIM, 1)
    zero = jnp.zeros_like(k)
    parts = (jnp.where(lo64, k, zero), jnp.where(lo64, zero, kr),
             jnp.where(lo64, kr, zero), jnp.where(lo64, zero, k))
    for i, part in enumerate(parts):
        o_ref[:, 1536 + LANES * i:1536 + LANES * (i + 1)] = part.astype(_BF16)
    vt_ref[...] = hb[:, 640:768].T.astype(_BF16)

    ha = jnp.dot(xb, wa_ref[...], preferred_element_type=_F32)
    o_ref[:, 0:512] = (ha[:, 0:512] * (Q_SCALE * LOG2E)).astype(_BF16)
    o_ref[:, 512:1024] = ha[:, 512:1024].astype(_BF16)
    na_vt = lax.dot_general(wvt_ref[...], xb, _NT, preferred_element_type=_F32).astype(_BF16)
    for t in range(na_vt_ref.shape[0]):
        na_vt_ref[t] = na_vt[:, NA_VT_CHUNK * t:NA_VT_CHUNK * (t + 1)]


def _proj0(x2d, wa, wvt, wb, cos, sin, gain, ones_bd, seq_len):
    rows = x2d.shape[0]
    tm = ROW_TILE
    pos_blocks = seq_len // tm
    na_chunks = tm // NA_VT_CHUNK
    na_w = NA_HEADS * HEAD_DIM
    return pl.pallas_call(
        _proj0_kernel,
        grid=(rows // tm,),
        in_specs=[
            pl.BlockSpec((tm, D_MODEL), lambda i: (i, 0)),
            _const_spec(wa.shape),
            _const_spec(wvt.shape),
            _const_spec(wb.shape),
            pl.BlockSpec((tm, LANES), lambda i: (i % pos_blocks, 0)),
            pl.BlockSpec((tm, LANES), lambda i: (i % pos_blocks, 0)),
            _const_spec(gain.shape),
            _const_spec(ones_bd.shape),
        ],
        out_specs=[pl.BlockSpec((tm, H0_W), lambda i: (i, 0)),
                   pl.BlockSpec((na_chunks, na_w, NA_VT_CHUNK), lambda i: (i, 0, 0)),
                   pl.BlockSpec((None, LANES, tm), lambda i: (i, 0, 0))],
        out_shape=[jax.ShapeDtypeStruct((rows, H0_W), _BF16),
                   jax.ShapeDtypeStruct((rows // NA_VT_CHUNK, na_w, NA_VT_CHUNK), _BF16),
                   jax.ShapeDtypeStruct((rows // tm, LANES, tm), _BF16)],
        compiler_params=_params("parallel"),
        name="proj0",
    )(x2d, wa, wvt, wb, cos, sin, gain, ones_bd)


H1_W = 2 * D_MODEL


def _proj1_kernel(x_ref, w_ref, wvt_ref, cos_ref, sin_ref, o_ref, vt_ref):
    xb = x_ref[...].astype(_BF16)
    cos = cos_ref[...]
    sin = sin_ref[...]
    for part, scale in ((0, Q_SCALE * LOG2E), (1, 1.0)):
        base = part * D_MODEL
        h = jnp.dot(xb, w_ref[:, base:base + D_MODEL], preferred_element_type=_F32)
        for g in range(D_MODEL // LANES):
            xg = h[:, LANES * g:LANES * (g + 1)]
            r = xg * cos + _rotate_pairs(xg, 32) * sin
            if scale != 1.0:
                r = r * scale
            o_ref[:, base + LANES * g:base + LANES * (g + 1)] = r.astype(_BF16)
    vt_ref[...] = lax.dot_general(wvt_ref[...], xb, _NT, preferred_element_type=_F32).astype(_BF16)


def _proj1(x2d, w_qk, w_vt, cos, sin, seq_len):
    rows = x2d.shape[0]
    tm = ROW_TILE
    pos_blocks = seq_len // tm
    return pl.pallas_call(
        _proj1_kernel,
        grid=(rows // tm,),
        in_specs=[
            pl.BlockSpec((tm, D_MODEL), lambda i: (i, 0)),
            _const_spec(w_qk.shape),
            _const_spec(w_vt.shape),
            pl.BlockSpec((tm, LANES), lambda i: (i % pos_blocks, 0)),
            pl.BlockSpec((tm, LANES), lambda i: (i % pos_blocks, 0)),
        ],
        out_specs=[pl.BlockSpec((tm, H1_W), lambda i: (i, 0)),
                   pl.BlockSpec((None, D_MODEL, tm), lambda i: (i, 0, 0))],
        out_shape=[jax.ShapeDtypeStruct((rows, H1_W), _BF16),
                   jax.ShapeDtypeStruct((rows // tm, D_MODEL, tm), _BF16)],
        compiler_params=_params("parallel"),
        name="proj1",
    )(x2d, w_qk, w_vt, cos, sin)


def _with_ones(vt):
    return jnp.concatenate([vt, jnp.ones((ONES_ROWS, vt.shape[1]), vt.dtype)], axis=0)


def _attend_block(n_k, n_chain, scores_fn, next_scores_fn, vt_fn, s_ref, acc_ref):
    m_state = [None] * n_chain
    for i in range(n_k):
        slot = i % 2
        for c in range(n_chain):
            s_ref[1 - slot, c] = scores_fn(c, i + 1) if i + 1 < n_k else next_scores_fn(c)
            s = s_ref[slot, c]
            m_next = jnp.max(s, axis=0, keepdims=True)
            if i > 0:
                m_next = jnp.maximum(m_state[c], m_next)
            p = jnp.exp2(s - m_next).astype(_BF16)
            pv = jnp.dot(_with_ones(vt_fn(c, i)), p, preferred_element_type=_F32)
            acc_ref[c] = pv if i == 0 else jnp.exp2(m_state[c] - m_next) * acc_ref[c] + pv
            m_state[c] = m_next


def _gqa_kernel(q_ref, k_ref, vt_ref, o_ref, s_ref, acc_ref, *, seq_len):
    tq, tk = ATT_TQ, ATT_TK
    n_q, n_k = seq_len // tq, seq_len // tk
    n_pairs = GQA_HEADS // 2
    d = HEAD_DIM

    def scores(qs, pair, e, i):
        kvh = pair // (n_pairs // GQA_KV_HEADS)
        col = 2 * LANES * kvh + LANES * e
        q = q_ref[pl.ds(qs, tq), LANES * pair:LANES * (pair + 1)]
        return lax.dot_general(k_ref[tk * i:tk * (i + 1), col:col + LANES], q, _NT, preferred_element_type=_F32)

    def issue_first(qs, pair):
        for e in range(2):
            s_ref[0, e] = scores(qs, pair, e, 0)

    def q_body(qi, carry):
        qs = pl.multiple_of(qi * tq, tq)
        qs_next = pl.multiple_of(jnp.minimum(qi + 1, n_q - 1) * tq, tq)
        for pair in range(n_pairs):
            kvh = pair // (n_pairs // GQA_KV_HEADS)
            nxt = (qs, pair + 1) if pair + 1 < n_pairs else (qs_next, 0)
            _attend_block(
                n_k, 2,
                lambda e, i, pair=pair: scores(qs, pair, e, i),
                lambda e, nxt=nxt: scores(nxt[0], nxt[1], e, 0),
                lambda e, i, kvh=kvh: vt_ref[i, d * kvh:d * (kvh + 1), :],
                s_ref, acc_ref)
            halves = [acc_ref[e, 0:d, :] * (1.0 / acc_ref[e, d:d + 1, :]) for e in range(2)]
            out = jnp.concatenate(halves, axis=0).T
            o_ref[pl.ds(qs, tq), LANES * pair:LANES * (pair + 1)] = out.astype(_BF16)
        return carry

    issue_first(0, 0)
    lax.fori_loop(0, n_q, q_body, 0)


def _gqa_attention(h0, vt0, batch, seq_len):
    h3 = h0.reshape(batch, seq_len, H0_W)
    width = GQA_HEADS * HEAD_DIM
    return pl.pallas_call(
        functools.partial(_gqa_kernel, seq_len=seq_len),
        grid=(batch,),
        in_specs=[
            pl.BlockSpec((None, seq_len, width), lambda b: (b, 0, 2)),
            pl.BlockSpec((None, seq_len, width), lambda b: (b, 0, 3)),
            pl.BlockSpec((seq_len // ATT_TK, LANES, ATT_TK), lambda b: (b, 0, 0)),
        ],
        out_specs=pl.BlockSpec((None, seq_len, width), lambda b: (b, 0, 0)),
        out_shape=jax.ShapeDtypeStruct((batch, seq_len, width), _BF16),
        scratch_shapes=[pltpu.VMEM((2, 2, ATT_TK, ATT_TQ), _F32),
                        pltpu.VMEM((2, HEAD_DIM + ONES_ROWS, ATT_TQ), _F32)],
        compiler_params=_params("parallel"),
        name="gqa_attention",
    )(h3, h3, vt0)


NA_BLOCK_ROWS = 4
NA_SLAB_ROWS = NA_WIN_H + NA_BLOCK_ROWS
NA_Q = NA_BLOCK_ROWS * GRID_W
NA_KEYS = NA_SLAB_ROWS * GRID_W
assert NA_Q == NA_VT_CHUNK and NA_KEYS % NA_VT_CHUNK == 0


def _na_kernel(q_ref, k_ref, vt_ref, bias_ref, o_ref, s_ref, *, n_rows):
    n_blocks = n_rows // NA_BLOCK_ROWS
    lo64 = lax.broadcasted_iota(jnp.int32, (NA_KEYS, LANES), 1) < HEAD_DIM
    d = HEAD_DIM

    def slab_start(bi):
        return jnp.clip(bi - 1, 0, n_blocks - NA_SLAB_ROWS // NA_BLOCK_ROWS)

    def scores(bi, e):
        q = q_ref[pl.ds(pl.multiple_of(bi * NA_Q, NA_Q), NA_Q), :]
        kslab = k_ref[pl.ds(pl.multiple_of(slab_start(bi) * NA_Q, NA_Q), NA_KEYS), :]
        zero = jnp.zeros_like(kslab)
        kk = jnp.where(lo64, kslab, zero) if e == 0 else jnp.where(lo64, zero, kslab)
        return lax.dot_general(kk, q, _NT, preferred_element_type=_F32)

    def attend(bi, e, slot):
        layout = jnp.where(bi == 0, 0, jnp.where(bi == n_blocks - 1, 2, 1))
        s = s_ref[slot, e] + bias_ref[e, layout]
        p = jnp.exp2(s - jnp.max(s, axis=0, keepdims=True)).astype(_BF16)
        acc = None
        for t in range(NA_KEYS // NA_VT_CHUNK):
            vt = vt_ref[slab_start(bi) + t, d * e:d * (e + 1), :]
            part = jnp.dot(_with_ones(vt), p[NA_VT_CHUNK * t:NA_VT_CHUNK * (t + 1), :],
                           preferred_element_type=_F32)
            acc = part if acc is None else acc + part
        return acc[0:d, :] * (1.0 / acc[d:d + 1, :])

    def two_blocks(j, carry):
        b0 = 2 * j
        blocks = [b0, b0 + 1, jnp.minimum(b0 + 2, n_blocks - 1)]
        for u in range(2):
            halves = []
            for e in range(2):
                s_ref[1 - u, e] = scores(blocks[u + 1], e)
                halves.append(attend(blocks[u], e, u))
            qs = pl.multiple_of(blocks[u] * NA_Q, NA_Q)
            o_ref[pl.ds(qs, NA_Q), :] = jnp.concatenate(halves, axis=0).T.astype(_BF16)
        return carry

    for e in range(2):
        s_ref[0, e] = scores(0, e)
    lax.fori_loop(0, n_blocks // 2, two_blocks, 0)


def _na_attention(h0, na_vt, bias, batch, seq_len):
    h3 = h0.reshape(batch, seq_len, H0_W)
    n_pairs = NA_HEADS // 2
    chunks = seq_len // NA_VT_CHUNK
    return pl.pallas_call(
        functools.partial(_na_kernel, n_rows=seq_len // GRID_W),
        grid=(n_pairs, batch),
        in_specs=[
            pl.BlockSpec((None, seq_len, LANES), lambda g, b: (b, 0, g)),
            pl.BlockSpec((None, seq_len, LANES), lambda g, b: (b, 0, n_pairs + g)),
            pl.BlockSpec((chunks, LANES, NA_VT_CHUNK), lambda g, b: (b, g, 0)),
            pl.BlockSpec((2, 3, NA_KEYS, NA_Q), lambda g, b: (g, 0, 0, 0)),
        ],
        out_specs=pl.BlockSpec((None, seq_len, LANES), lambda g, b: (b, 0, g)),
        out_shape=jax.ShapeDtypeStruct((batch, seq_len, NA_HEADS * HEAD_DIM), _BF16),
        scratch_shapes=[pltpu.VMEM((2, 2, NA_KEYS, NA_Q), _F32)],
        compiler_params=_params("parallel", "parallel"),
        name="na_attention",
    )(h3, h3, na_vt, bias)


def _na_bias_table(rpb):
    cols = np.arange(GRID_W)
    col_start = np.clip(cols - NA_WIN_W // 2, 0, GRID_W - NA_WIN_W)
    kc = np.arange(GRID_W)
    in_cols = (kc[:, None] >= col_start[None, :]) & (kc[:, None] < col_start[None, :] + NA_WIN_W)
    col_idx = kc[:, None] - cols[None, :] + (NA_WIN_W - 1)
    i = np.arange(NA_BLOCK_ROWS)
    j = np.arange(NA_SLAB_ROWS)
    slab_off = np.array([0, -NA_WIN_H // 2, -NA_WIN_H])
    win_off = np.stack([np.zeros_like(i), i - NA_WIN_H // 2, np.full_like(i, -NA_WIN_H // 2)])
    key_row = slab_off[:, None] + j[None, :]
    in_rows = ((key_row[:, :, None] >= win_off[:, None, :])
               & (key_row[:, :, None] < win_off[:, None, :] + NA_WIN_H))
    row_idx = key_row[:, :, None] - i[None, None, :] + (NA_WIN_H - 1)
    row_pick = (in_rows[..., None] & (row_idx[..., None] == np.arange(2 * NA_WIN_H - 1))).astype(np.float32)
    col_pick = (in_cols[..., None] & (col_idx[..., None] == np.arange(2 * NA_WIN_W - 1))).astype(np.float32)
    by_col = jnp.einsum("hab,kcb->hakc", rpb, col_pick, precision=lax.Precision.HIGHEST)
    table = jnp.einsum("ljia,hakc->hljkic", row_pick, by_col, precision=lax.Precision.HIGHEST)
    valid = in_rows[:, :, None, :, None] & in_cols[None, None, :, None, :]
    table = jnp.where(valid[None], table * LOG2E, MASK_VALUE)
    return table.reshape(rpb.shape[0], 3, NA_KEYS, NA_Q).astype(_F32)


def _diff_kernel(q_ref, k_ref, vt_ref, lq1_ref, lk1_ref, lq2_ref, lk2_ref, g_ref, o_ref, s_ref, acc_ref,
                 *, seq_len, lambda_init):
    tq, tk = ATT_TQ, ATT_TK
    n_q, n_k = seq_len // tq, seq_len // tk
    d = 2 * HEAD_DIM
    lam = (jnp.exp(jnp.sum(lq1_ref[...] * lk1_ref[...], axis=1, keepdims=True))
           - jnp.exp(jnp.sum(lq2_ref[...] * lk2_ref[...], axis=1, keepdims=True)) + lambda_init)
    lo64 = lax.broadcasted_iota(jnp.int32, (tk, LANES), 1) < HEAD_DIM
    gain = g_ref[...] * (1.0 - lambda_init)

    def scores(qs, c, i):
        kblk = k_ref[tk * i:tk * (i + 1), :]
        zero = jnp.zeros_like(kblk)
        kk = jnp.where(lo64, kblk, zero) if c == 0 else jnp.where(lo64, zero, kblk)
        return lax.dot_general(kk, q_ref[pl.ds(qs, tq), :], _NT, preferred_element_type=_F32)

    def q_body(qi, carry):
        qs = pl.multiple_of(qi * tq, tq)
        qs_next = pl.multiple_of(jnp.minimum(qi + 1, n_q - 1) * tq, tq)
        _attend_block(n_k, 2, lambda c, i: scores(qs, c, i), lambda c: scores(qs_next, c, 0),
                      lambda c, i: vt_ref[i], s_ref, acc_ref)
        o1 = acc_ref[0, 0:d, :] * (1.0 / acc_ref[0, d:d + 1, :])
        o2 = acc_ref[1, 0:d, :] * (1.0 / acc_ref[1, d:d + 1, :])
        o = (o1 - lam * o2).T
        ms = jnp.mean(o * o, axis=1, keepdims=True)
        o_ref[pl.ds(qs, tq), :] = (o * lax.rsqrt(ms + SUBLN_EPS) * gain).astype(_BF16)
        return carry

    for c in range(2):
        s_ref[0, c] = scores(0, c, 0)
    lax.fori_loop(0, n_q, q_body, 0, unroll=2)


def _diff_attention(h1, vt1, lq1, lk1, lq2, lk2, g_subln, lambda_init, batch, seq_len):
    h3 = h1.reshape(batch, seq_len, H1_W)
    heads = DIFF_HEADS
    small = lambda n: pl.BlockSpec((1, n), lambda b, h: (0, 0))
    return pl.pallas_call(
        functools.partial(_diff_kernel, seq_len=seq_len, lambda_init=lambda_init),
        grid=(batch, heads),
        in_specs=[
            pl.BlockSpec((None, seq_len, LANES), lambda b, h: (b, 0, h)),
            pl.BlockSpec((None, seq_len, LANES), lambda b, h: (b, 0, heads + h)),
            pl.BlockSpec((seq_len // ATT_TK, LANES, ATT_TK), lambda b, h: (b, h, 0)),
            small(HEAD_DIM), small(HEAD_DIM), small(HEAD_DIM), small(HEAD_DIM), small(2 * HEAD_DIM),
        ],
        out_specs=pl.BlockSpec((None, seq_len, LANES), lambda b, h: (b, 0, h)),
        out_shape=jax.ShapeDtypeStruct((batch, seq_len, D_MODEL), _BF16),
        scratch_shapes=[pltpu.VMEM((2, 2, ATT_TK, ATT_TQ), _F32),
                        pltpu.VMEM((2, 2 * HEAD_DIM + ONES_ROWS, ATT_TQ), _F32)],
        compiler_params=_params("parallel", "parallel"),
        name="diff_attention",
    )(h3, h3, vt1, lq1, lk1, lq2, lk2, g_subln)


def _layer_tail_kernel(*refs, n_parts):
    part_refs = refs[:n_parts]
    (x_ref, wo_ref, g1_ref, b1_ref, wg_ref, wu_ref, wd_ref, g2_ref, b2_ref, o_ref) = refs[n_parts:]
    if n_parts == 1:
        a = part_refs[0][...]
    else:
        a = jnp.concatenate([r[...] for r in part_refs], axis=1)
    mix = jnp.dot(a, wo_ref[...], preferred_element_type=_F32)
    x1 = _layer_norm(ALPHA * x_ref[...] + mix, g1_ref[...], b1_ref[...])
    xb = x1.astype(_BF16)
    hidden = []
    for c in range(N_FF_CHUNKS):
        gate = jnp.dot(xb, wg_ref[c], preferred_element_type=_F32)
        up = jnp.dot(xb, wu_ref[c], preferred_element_type=_F32)
        hidden.append((gate * jax.nn.sigmoid(gate) * up).astype(_BF16))
    ffn = jnp.dot(jnp.concatenate(hidden, axis=1), wd_ref[...], preferred_element_type=_F32)
    o_ref[...] = _layer_norm(ALPHA * x1 + ffn, g2_ref[...], b2_ref[...])


def _layer_tail(parts, x2d, w_out, g1, b1, wg, wu, wd, g2, b2):
    rows = x2d.shape[0]
    tm = ROW_TILE
    part_specs = [pl.BlockSpec((tm, p.shape[1]), lambda i: (i, 0)) for p in parts]
    consts = (w_out, g1, b1, wg, wu, wd, g2, b2)
    return pl.pallas_call(
        functools.partial(_layer_tail_kernel, n_parts=len(parts)),
        grid=(rows // tm,),
        in_specs=part_specs + [pl.BlockSpec((tm, D_MODEL), lambda i: (i, 0))] + [_const_spec(c.shape) for c in consts],
        out_specs=pl.BlockSpec((tm, D_MODEL), lambda i: (i, 0)),
        out_shape=jax.ShapeDtypeStruct((rows, D_MODEL), _F32),
        compiler_params=_params("parallel"),
        name="layer_tail",
    )(*parts, x2d, *consts)


def _rope_tables(angles):
    cos = jnp.cos(angles)
    sin = jnp.sin(angles)
    return jnp.concatenate([cos, cos], axis=1), jnp.concatenate([-sin, sin], axis=1)


def _axial_tables(seq_len):
    t = jnp.arange(seq_len)
    dim = HEAD_DIM // 2
    inv = ROPE_THETA ** (-jnp.arange(0, dim, 2, dtype=_F32) / dim)
    row_c, row_s = _rope_tables((t // GRID_W).astype(_F32)[:, None] * inv[None, :])
    col_c, col_s = _rope_tables((t % GRID_W).astype(_F32)[:, None] * inv[None, :])
    cos = jnp.concatenate([row_c, col_c] * (LANES // HEAD_DIM), axis=1)
    sin = jnp.concatenate([row_s, col_s] * (LANES // HEAD_DIM), axis=1)
    return cos, sin


def _seq_tables(seq_len):
    t = jnp.arange(seq_len)
    inv = ROPE_THETA ** (-jnp.arange(0, HEAD_DIM, 2, dtype=_F32) / HEAD_DIM)
    cos, sin = _rope_tables(t.astype(_F32)[:, None] * inv[None, :])
    reps = LANES // HEAD_DIM
    return jnp.concatenate([cos] * reps, axis=1), jnp.concatenate([sin] * reps, axis=1)


def _ffn_weights(wg, wu, wd):
    chunked = lambda w: jnp.transpose(w.reshape(D_MODEL, N_FF_CHUNKS, FF_CHUNK), (1, 0, 2)).astype(_BF16)
    return chunked(wg), chunked(wu), wd.astype(_BF16)


def _trunk(x, p):
    batch, seq_len, _ = x.shape
    x2d = x.reshape(batch * seq_len, D_MODEL)
    row = lambda v: v.reshape(1, -1)

    h0, na_vt, vt0 = _proj0(x2d, p["w_in0_a"], p["w_in0_vt"], p["w_in0_b"], p["ax_cos"][:seq_len],
                            p["ax_sin"][:seq_len], p["gqa_gain"], p["ones_bd"], seq_len)
    a_out = _na_attention(h0, na_vt, p["na_bias"], batch, seq_len).reshape(batch * seq_len, -1)
    b_out = _gqa_attention(h0, vt0, batch, seq_len).reshape(batch * seq_len, -1)
    x2d = _layer_tail([a_out, b_out], x2d, p["w_out0"], row(p["ln_mix_g"][0]), row(p["ln_mix_b"][0]),
                      *p["ffn0"], row(p["ln_ffn_g"][0]), row(p["ln_ffn_b"][0]))

    h1, vt1 = _proj1(x2d, p["w_in1_qk"], p["w_in1_vt"], p["seq_cos"][:seq_len], p["seq_sin"][:seq_len], seq_len)
    lambda_init = 0.8 - 0.6 * math.exp(-0.3 * 1)
    c_out = _diff_attention(h1, vt1, p["lq1"], p["lk1"], p["lq2"], p["lk2"], p["g_subln"],
                            lambda_init, batch, seq_len).reshape(batch * seq_len, -1)
    x2d = _layer_tail([c_out], x2d, p["w_out1"], row(p["ln_mix_g"][1]), row(p["ln_mix_b"][1]),
                      *p["ffn1"], row(p["ln_ffn_g"][1]), row(p["ln_ffn_b"][1]))
    return x2d.reshape(batch, seq_len, D_MODEL)


def kernel(x_prompt, x_sample, w_in_mix0, rpb_na, g_q_gqa, g_k_gqa, w_out_mix0, w_in_mix1, lam_q1, lam_k1,
           lam_q2, lam_k2, g_subln, w_out_mix1, ln_mix_g, ln_mix_b, w_ffn_gate, w_ffn_up, w_ffn_down,
           ln_ffn_g, ln_ffn_b):
    max_len = max(x_prompt.shape[1], x_sample.shape[1])
    ax_cos, ax_sin = _axial_tables(max_len)
    seq_cos, seq_sin = _seq_tables(max_len)
    head_of_lane = jnp.arange(LANES) // HEAD_DIM
    w0 = w_in_mix0[0].astype(_BF16)
    na_w = NA_HEADS * HEAD_DIM
    p = {
        "w_in0_a": w0[:, :2 * na_w],
        "w_in0_vt": w0[:, 2 * na_w:3 * na_w].T,
        "w_in0_b": w0[:, 3 * na_w:],
        "ax_cos": ax_cos, "ax_sin": ax_sin, "seq_cos": seq_cos, "seq_sin": seq_sin,
        "gqa_gain": jnp.concatenate([jnp.tile(g_q_gqa[0], GQA_HEADS), jnp.tile(g_k_gqa[0], GQA_KV_HEADS)]
                                    ).reshape(1, -1).astype(_F32),
        "ones_bd": (head_of_lane[:, None] == head_of_lane[None, :]).astype(_BF16),
        "na_bias": _na_bias_table(rpb_na[0]),
        "w_out0": w_out_mix0[0].astype(_BF16),
        "ffn0": _ffn_weights(w_ffn_gate[0], w_ffn_up[0], w_ffn_down[0]),
        "w_in1_qk": w_in_mix1[0][:, :2 * D_MODEL].astype(_BF16),
        "w_in1_vt": w_in_mix1[0][:, 2 * D_MODEL:].T.astype(_BF16),
        "lq1": lam_q1[0].reshape(1, -1), "lk1": lam_k1[0].reshape(1, -1),
        "lq2": lam_q2[0].reshape(1, -1), "lk2": lam_k2[0].reshape(1, -1),
        "g_subln": g_subln[0].reshape(1, -1),
        "w_out1": w_out_mix1[0].astype(_BF16),
        "ffn1": _ffn_weights(w_ffn_gate[1], w_ffn_up[1], w_ffn_down[1]),
        "ln_mix_g": ln_mix_g, "ln_mix_b": ln_mix_b, "ln_ffn_g": ln_ffn_g, "ln_ffn_b": ln_ffn_b,
    }
    return _trunk(x_prompt, p), _trunk(x_sample, p)
```

```python
import functools
import math

import jax
import jax.numpy as jnp
import numpy as np
from jax import lax
from jax.experimental import pallas as pl
from jax.experimental.pallas import tpu as pltpu

D_MODEL = 1024
DEPTH = 2
GRID_W = 64
HEAD_DIM = 64
NA_HEADS = 8
GQA_HEADS = 8
GQA_KV_HEADS = 2
DIFF_HEADS = 8
NA_WIN_H = 8
NA_WIN_W = 16
ROPE_THETA = 10000.0
D_FF = 2816
LN_EPS = 1e-5
RMS_EPS = 1e-6
SUBLN_EPS = 1e-5
ALPHA = (2.0 * DEPTH) ** 0.25
Q_SCALE = HEAD_DIM ** -0.5
LOG2E = math.log2(math.e)

LANES = 128
FF_CHUNK = 256
N_FF_CHUNKS = D_FF // FF_CHUNK
ROW_TILE = 512
ATT_TQ = 512
ATT_TK = 512
ONES_ROWS = 16
assert ATT_TK == ROW_TILE
MASK_VALUE = -1e30
VMEM_LIMIT = 56 * 1024 * 1024

_BF16 = jnp.bfloat16
_F32 = jnp.float32
_NT = (((1,), (1,)), ((), ()))


def _params(*semantics):
    return pltpu.CompilerParams(dimension_semantics=semantics, vmem_limit_bytes=VMEM_LIMIT)


def _const_spec(shape):
    zeros = (0,) * len(shape)
    return pl.BlockSpec(shape, lambda *_: zeros, pipeline_mode=pl.Buffered(1))


def _layer_norm(xf, gain, bias):
    mu = jnp.mean(xf, axis=-1, keepdims=True)
    xc = xf - mu
    var = jnp.mean(xc * xc, axis=-1, keepdims=True)
    return xc * lax.rsqrt(var + LN_EPS) * gain + bias


def _rotate_pairs(x, half):
    lane = lax.broadcasted_iota(jnp.int32, x.shape, 1)
    first = (lane % (2 * half)) < half
    return jnp.where(first, pltpu.roll(x, LANES - half, 1), pltpu.roll(x, half, 1))


H0_W = 2048
NA_VT_CHUNK = 256


def _proj0_kernel(x_ref, wa_ref, wvt_ref, wb_ref, cos_ref, sin_ref, gain_ref, ones_ref, o_ref, na_vt_ref, vt_ref):
    xb = x_ref[...].astype(_BF16)
    hb = jnp.dot(xb, wb_ref[...], preferred_element_type=_F32)
    cos = cos_ref[...]
    sin = sin_ref[...]
    ones_bd = ones_ref[...]
    lo64 = lax.broadcasted_iota(jnp.int32, cos.shape, 1) < HEAD_DIM

    def norm_rope(xg, gain):
        ssq = jnp.dot((xg * xg).astype(_BF16), ones_bd, preferred_element_type=_F32)
        n = xg * lax.rsqrt(ssq * (1.0 / HEAD_DIM) + RMS_EPS) * gain
        return n * cos + _rotate_pairs(n, 16) * sin

    for g in range(4):
        sl = slice(LANES * g, LANES * (g + 1))
        qg = norm_rope(hb[:, sl], gain_ref[:, sl]) * (Q_SCALE * LOG2E)
        o_ref[:, 1024 + LANES * g:1024 + LANES * (g + 1)] = qg.astype(_BF16)

    k = norm_rope(hb[:, 512:640], gain_ref[:, 512:640])
    kr = pltpu.roll(k, HEAD_DIM, 1)
    zero = jnp.zeros_like(k)
    parts = (jnp.where(lo64, k, zero), jnp.where(lo64, zero, kr),
             jnp.where(lo64, kr, zero), jnp.where(lo64, zero, k))
    for i, part in enumerate(parts):
        o_ref[:, 1536 + LANES * i:1536 + LANES * (i + 1)] = part.astype(_BF16)
    vt_ref[...] = hb[:, 640:768].T.astype(_BF16)

    ha = jnp.dot(xb, wa_ref[...], preferred_element_type=_F32)
    o_ref[:, 0:512] = (ha[:, 0:512] * (Q_SCALE * LOG2E)).astype(_BF16)
    o_ref[:, 512:1024] = ha[:, 512:1024].astype(_BF16)
    na_vt = lax.dot_general(wvt_ref[...], xb, _NT, preferred_element_type=_F32).astype(_BF16)
    for t in range(na_vt_ref.shape[0]):
        na_vt_ref[t] = na_vt[:, NA_VT_CHUNK * t:NA_VT_CHUNK * (t + 1)]


def _proj0(x2d, wa, wvt, wb, cos, sin, gain, ones_bd, seq_len):
    rows = x2d.shape[0]
    tm = ROW_TILE
    pos_blocks = seq_len // tm
    na_chunks = tm // NA_VT_CHUNK
    na_w = NA_HEADS * HEAD_DIM
    return pl.pallas_call(
        _proj0_kernel,
        grid=(rows // tm,),
        in_specs=[
            pl.BlockSpec((tm, D_MODEL), lambda i: (i, 0)),
            _const_spec(wa.shape),
            _const_spec(wvt.shape),
            _const_spec(wb.shape),
            pl.BlockSpec((tm, LANES), lambda i: (i % pos_blocks, 0)),
            pl.BlockSpec((tm, LANES), lambda i: (i % pos_blocks, 0)),
            _const_spec(gain.shape),
            _const_spec(ones_bd.shape),
        ],
        out_specs=[pl.BlockSpec((tm, H0_W), lambda i: (i, 0)),
                   pl.BlockSpec((na_chunks, na_w, NA_VT_CHUNK), lambda i: (i, 0, 0)),
                   pl.BlockSpec((None, LANES, tm), lambda i: (i, 0, 0))],
        out_shape=[jax.ShapeDtypeStruct((rows, H0_W), _BF16),
                   jax.ShapeDtypeStruct((rows // NA_VT_CHUNK, na_w, NA_VT_CHUNK), _BF16),
                   jax.ShapeDtypeStruct((rows // tm, LANES, tm), _BF16)],
        compiler_params=_params("parallel"),
        name="proj0",
    )(x2d, wa, wvt, wb, cos, sin, gain, ones_bd)


H1_W = 2 * D_MODEL


def _proj1_kernel(x_ref, w_ref, wvt_ref, cos_ref, sin_ref, o_ref, vt_ref):
    xb = x_ref[...].astype(_BF16)
    cos = cos_ref[...]
    sin = sin_ref[...]
    for part, scale in ((0, Q_SCALE * LOG2E), (1, 1.0)):
        base = part * D_MODEL
        h = jnp.dot(xb, w_ref[:, base:base + D_MODEL], preferred_element_type=_F32)
        for g in range(D_MODEL // LANES):
            xg = h[:, LANES * g:LANES * (g + 1)]
            r = xg * cos + _rotate_pairs(xg, 32) * sin
            if scale != 1.0:
                r = r * scale
            o_ref[:, base + LANES * g:base + LANES * (g + 1)] = r.astype(_BF16)
    vt_ref[...] = lax.dot_general(wvt_ref[...], xb, _NT, preferred_element_type=_F32).astype(_BF16)


def _proj1(x2d, w_qk, w_vt, cos, sin, seq_len):
    rows = x2d.shape[0]
    tm = ROW_TILE
    pos_blocks = seq_len // tm
    return pl.pallas_call(
        _proj1_kernel,
        grid=(rows // tm,),
        in_specs=[
            pl.BlockSpec((tm, D_MODEL), lambda i: (i, 0)),
            _const_spec(w_qk.shape),
            _const_spec(w_vt.shape),
            pl.BlockSpec((tm, LANES), lambda i: (i % pos_blocks, 0)),
            pl.BlockSpec((tm, LANES), lambda i: (i % pos_blocks, 0)),
        ],
        out_specs=[pl.BlockSpec((tm, H1_W), lambda i: (i, 0)),
                   pl.BlockSpec((None, D_MODEL, tm), lambda i: (i, 0, 0))],
        out_shape=[jax.ShapeDtypeStruct((rows, H1_W), _BF16),
                   jax.ShapeDtypeStruct((rows // tm, D_MODEL, tm), _BF16)],
        compiler_params=_params("parallel"),
        name="proj1",
    )(x2d, w_qk, w_vt, cos, sin)


def _with_ones(vt):
    return jnp.concatenate([vt, jnp.ones((ONES_ROWS, vt.shape[1]), vt.dtype)], axis=0)


def _issue_scores(s_ref, slot, c, s):
    s_ref[slot, c] = s
    return jnp.max(s, axis=0, keepdims=True)


def _attend_block(n_k, n_chain, scores_fn, next_scores_fn, vt_fn, s_ref, acc_ref, chunk_max):
    chunk_max = list(chunk_max)
    m_state = [None] * n_chain
    for i in range(n_k):
        slot = i % 2
        for c in range(n_chain):
            s_next = scores_fn(c, i + 1) if i + 1 < n_k else next_scores_fn(c)
            next_max = _issue_scores(s_ref, 1 - slot, c, s_next)
            m_next = chunk_max[c] if i == 0 else jnp.maximum(m_state[c], chunk_max[c])
            p = jnp.exp2(s_ref[slot, c] - m_next).astype(_BF16)
            pv = jnp.dot(_with_ones(vt_fn(c, i)), p, preferred_element_type=_F32)
            acc_ref[c] = pv if i == 0 else jnp.exp2(m_state[c] - m_next) * acc_ref[c] + pv
            m_state[c] = m_next
            chunk_max[c] = next_max
    return tuple(chunk_max)


def _gqa_kernel(q_ref, k_ref, vt_ref, o_ref, s_ref, acc_ref, *, seq_len):
    tq, tk = ATT_TQ, ATT_TK
    n_q, n_k = seq_len // tq, seq_len // tk
    n_pairs = GQA_HEADS // 2
    d = HEAD_DIM

    def scores(qs, pair, e, i):
        kvh = pair // (n_pairs // GQA_KV_HEADS)
        col = 2 * LANES * kvh + LANES * e
        q = q_ref[pl.ds(qs, tq), LANES * pair:LANES * (pair + 1)]
        return lax.dot_general(k_ref[tk * i:tk * (i + 1), col:col + LANES], q, _NT, preferred_element_type=_F32)

    def q_body(qi, chunk_max):
        qs = pl.multiple_of(qi * tq, tq)
        qs_next = pl.multiple_of(jnp.minimum(qi + 1, n_q - 1) * tq, tq)
        for pair in range(n_pairs):
            kvh = pair // (n_pairs // GQA_KV_HEADS)
            nxt = (qs, pair + 1) if pair + 1 < n_pairs else (qs_next, 0)
            chunk_max = _attend_block(
                n_k, 2,
                lambda e, i, pair=pair: scores(qs, pair, e, i),
                lambda e, nxt=nxt: scores(nxt[0], nxt[1], e, 0),
                lambda e, i, kvh=kvh: vt_ref[i, d * kvh:d * (kvh + 1), :],
                s_ref, acc_ref, chunk_max)
            halves = [acc_ref[e, 0:d, :] * (1.0 / acc_ref[e, d:d + 1, :]) for e in range(2)]
            out = jnp.concatenate(halves, axis=0).T
            o_ref[pl.ds(qs, tq), LANES * pair:LANES * (pair + 1)] = out.astype(_BF16)
        return chunk_max

    first = tuple(_issue_scores(s_ref, 0, e, scores(0, 0, e, 0)) for e in range(2))
    lax.fori_loop(0, n_q, q_body, first)


def _gqa_attention(h0, vt0, batch, seq_len):
    h3 = h0.reshape(batch, seq_len, H0_W)
    width = GQA_HEADS * HEAD_DIM
    return pl.pallas_call(
        functools.partial(_gqa_kernel, seq_len=seq_len),
        grid=(batch,),
        in_specs=[
            pl.BlockSpec((None, seq_len, width), lambda b: (b, 0, 2)),
            pl.BlockSpec((None, seq_len, width), lambda b: (b, 0, 3)),
            pl.BlockSpec((seq_len // ATT_TK, LANES, ATT_TK), lambda b: (b, 0, 0)),
        ],
        out_specs=pl.BlockSpec((None, seq_len, width), lambda b: (b, 0, 0)),
        out_shape=jax.ShapeDtypeStruct((batch, seq_len, width), _BF16),
        scratch_shapes=[pltpu.VMEM((2, 2, ATT_TK, ATT_TQ), _F32),
                        pltpu.VMEM((2, HEAD_DIM + ONES_ROWS, ATT_TQ), _F32)],
        compiler_params=_params("parallel"),
        name="gqa_attention",
    )(h3, h3, vt0)


NA_BLOCK_ROWS = 4
NA_SLAB_ROWS = NA_WIN_H + NA_BLOCK_ROWS
NA_Q = NA_BLOCK_ROWS * GRID_W
NA_KEYS = NA_SLAB_ROWS * GRID_W
assert NA_Q == NA_VT_CHUNK and NA_KEYS % NA_VT_CHUNK == 0


def _na_kernel(q_ref, k_ref, vt_ref, bias_ref, o_ref, s_ref, *, n_rows):
    n_blocks = n_rows // NA_BLOCK_ROWS
    lo64 = lax.broadcasted_iota(jnp.int32, (NA_KEYS, LANES), 1) < HEAD_DIM
    d = HEAD_DIM

    def slab_start(bi):
        return jnp.clip(bi - 1, 0, n_blocks - NA_SLAB_ROWS // NA_BLOCK_ROWS)

    def scores(bi, e):
        q = q_ref[pl.ds(pl.multiple_of(bi * NA_Q, NA_Q), NA_Q), :]
        kslab = k_ref[pl.ds(pl.multiple_of(slab_start(bi) * NA_Q, NA_Q), NA_KEYS), :]
        zero = jnp.zeros_like(kslab)
        kk = jnp.where(lo64, kslab, zero) if e == 0 else jnp.where(lo64, zero, kslab)
        return lax.dot_general(kk, q, _NT, preferred_element_type=_F32)

    def attend(bi, e, slot):
        layout = jnp.where(bi == 0, 0, jnp.where(bi == n_blocks - 1, 2, 1))
        s = s_ref[slot, e] + bias_ref[e, layout]
        p = jnp.exp2(s - jnp.max(s, axis=0, keepdims=True)).astype(_BF16)
        acc = None
        for t in range(NA_KEYS // NA_VT_CHUNK):
            vt = vt_ref[slab_start(bi) + t, d * e:d * (e + 1), :]
            part = jnp.dot(_with_ones(vt), p[NA_VT_CHUNK * t:NA_VT_CHUNK * (t + 1), :],
                           preferred_element_type=_F32)
            acc = part if acc is None else acc + part
        return acc[0:d, :] * (1.0 / acc[d:d + 1, :])

    def two_blocks(j, carry):
        b0 = 2 * j
        blocks = [b0, b0 + 1, jnp.minimum(b0 + 2, n_blocks - 1)]
        for u in range(2):
            halves = []
            for e in range(2):
                s_ref[1 - u, e] = scores(blocks[u + 1], e)
                halves.append(attend(blocks[u], e, u))
            qs = pl.multiple_of(blocks[u] * NA_Q, NA_Q)
            o_ref[pl.ds(qs, NA_Q), :] = jnp.concatenate(halves, axis=0).T.astype(_BF16)
        return carry

    for e in range(2):
        s_ref[0, e] = scores(0, e)
    lax.fori_loop(0, n_blocks // 2, two_blocks, 0)


def _na_attention(h0, na_vt, bias, batch, seq_len):
    h3 = h0.reshape(batch, seq_len, H0_W)
    n_pairs = NA_HEADS // 2
    chunks = seq_len // NA_VT_CHUNK
    return pl.pallas_call(
        functools.partial(_na_kernel, n_rows=seq_len // GRID_W),
        grid=(n_pairs, batch),
        in_specs=[
            pl.BlockSpec((None, seq_len, LANES), lambda g, b: (b, 0, g)),
            pl.BlockSpec((None, seq_len, LANES), lambda g, b: (b, 0, n_pairs + g)),
            pl.BlockSpec((chunks, LANES, NA_VT_CHUNK), lambda g, b: (b, g, 0)),
            pl.BlockSpec((2, 3, NA_KEYS, NA_Q), lambda g, b: (g, 0, 0, 0)),
        ],
        out_specs=pl.BlockSpec((None, seq_len, LANES), lambda g, b: (b, 0, g)),
        out_shape=jax.ShapeDtypeStruct((batch, seq_len, NA_HEADS * HEAD_DIM), _BF16),
        scratch_shapes=[pltpu.VMEM((2, 2, NA_KEYS, NA_Q), _F32)],
        compiler_params=_params("parallel", "parallel"),
        name="na_attention",
    )(h3, h3, na_vt, bias)


def _na_bias_table(rpb):
    cols = np.arange(GRID_W)
    col_start = np.clip(cols - NA_WIN_W // 2, 0, GRID_W - NA_WIN_W)
    kc = np.arange(GRID_W)
    in_cols = (kc[:, None] >= col_start[None, :]) & (kc[:, None] < col_start[None, :] + NA_WIN_W)
    col_idx = kc[:, None] - cols[None, :] + (NA_WIN_W - 1)
    i = np.arange(NA_BLOCK_ROWS)
    j = np.arange(NA_SLAB_ROWS)
    slab_off = np.array([0, -NA_WIN_H // 2, -NA_WIN_H])
    win_off = np.stack([np.zeros_like(i), i - NA_WIN_H // 2, np.full_like(i, -NA_WIN_H // 2)])
    key_row = slab_off[:, None] + j[None, :]
    in_rows = ((key_row[:, :, None] >= win_off[:, None, :])
               & (key_row[:, :, None] < win_off[:, None, :] + NA_WIN_H))
    row_idx = key_row[:, :, None] - i[None, None, :] + (NA_WIN_H - 1)
    row_pick = (in_rows[..., None] & (row_idx[..., None] == np.arange(2 * NA_WIN_H - 1))).astype(np.float32)
    col_pick = (in_cols[..., None] & (col_idx[..., None] == np.arange(2 * NA_WIN_W - 1))).astype(np.float32)
    by_col = jnp.einsum("hab,kcb->hakc", rpb, col_pick, precision=lax.Precision.HIGHEST)
    table = jnp.einsum("ljia,hakc->hljkic", row_pick, by_col, precision=lax.Precision.HIGHEST)
    valid = in_rows[:, :, None, :, None] & in_cols[None, None, :, None, :]
    table = jnp.where(valid[None], table * LOG2E, MASK_VALUE)
    return table.reshape(rpb.shape[0], 3, NA_KEYS, NA_Q).astype(_F32)


def _diff_kernel(q_ref, k_ref, vt_ref, lq1_ref, lk1_ref, lq2_ref, lk2_ref, g_ref, o_ref, s_ref, acc_ref,
                 *, seq_len, lambda_init):
    tq, tk = ATT_TQ, ATT_TK
    n_q, n_k = seq_len // tq, seq_len // tk
    d = 2 * HEAD_DIM
    lam = (jnp.exp(jnp.sum(lq1_ref[...] * lk1_ref[...], axis=1, keepdims=True))
           - jnp.exp(jnp.sum(lq2_ref[...] * lk2_ref[...], axis=1, keepdims=True)) + lambda_init)
    lo64 = lax.broadcasted_iota(jnp.int32, (tk, LANES), 1) < HEAD_DIM
    gain = g_ref[...] * (1.0 - lambda_init)

    def scores(qs, c, i):
        kblk = k_ref[tk * i:tk * (i + 1), :]
        zero = jnp.zeros_like(kblk)
        kk = jnp.where(lo64, kblk, zero) if c == 0 else jnp.where(lo64, zero, kblk)
        return lax.dot_general(kk, q_ref[pl.ds(qs, tq), :], _NT, preferred_element_type=_F32)

    def q_body(qi, chunk_max):
        qs = pl.multiple_of(qi * tq, tq)
        qs_next = pl.multiple_of(jnp.minimum(qi + 1, n_q - 1) * tq, tq)
        chunk_max = _attend_block(n_k, 2, lambda c, i: scores(qs, c, i), lambda c: scores(qs_next, c, 0),
                                  lambda c, i: vt_ref[i], s_ref, acc_ref, chunk_max)
        o1 = acc_ref[0, 0:d, :] * (1.0 / acc_ref[0, d:d + 1, :])
        o2 = acc_ref[1, 0:d, :] * (1.0 / acc_ref[1, d:d + 1, :])
        o = (o1 - lam * o2).T
        ms = jnp.mean(o * o, axis=1, keepdims=True)
        o_ref[pl.ds(qs, tq), :] = (o * lax.rsqrt(ms + SUBLN_EPS) * gain).astype(_BF16)
        return chunk_max

    first = tuple(_issue_scores(s_ref, 0, c, scores(0, c, 0)) for c in range(2))
    lax.fori_loop(0, n_q, q_body, first, unroll=2)


def _diff_attention(h1, vt1, lq1, lk1, lq2, lk2, g_subln, lambda_init, batch, seq_len):
    h3 = h1.reshape(batch, seq_len, H1_W)
    heads = DIFF_HEADS
    small = lambda n: pl.BlockSpec((1, n), lambda b, h: (0, 0))
    return pl.pallas_call(
        functools.partial(_diff_kernel, seq_len=seq_len, lambda_init=lambda_init),
        grid=(batch, heads),
        in_specs=[
            pl.BlockSpec((None, seq_len, LANES), lambda b, h: (b, 0, h)),
            pl.BlockSpec((None, seq_len, LANES), lambda b, h: (b, 0, heads + h)),
            pl.BlockSpec((seq_len // ATT_TK, LANES, ATT_TK), lambda b, h: (b, h, 0)),
            small(HEAD_DIM), small(HEAD_DIM), small(HEAD_DIM), small(HEAD_DIM), small(2 * HEAD_DIM),
        ],
        out_specs=pl.BlockSpec((None, seq_len, LANES), lambda b, h: (b, 0, h)),
        out_shape=jax.ShapeDtypeStruct((batch, seq_len, D_MODEL), _BF16),
        scratch_shapes=[pltpu.VMEM((2, 2, ATT_TK, ATT_TQ), _F32),
                        pltpu.VMEM((2, 2 * HEAD_DIM + ONES_ROWS, ATT_TQ), _F32)],
        compiler_params=_params("parallel", "parallel"),
        name="diff_attention",
    )(h3, h3, vt1, lq1, lk1, lq2, lk2, g_subln)


def _layer_tail_kernel(*refs, n_parts):
    part_refs = refs[:n_parts]
    (x_ref, wo_ref, g1_ref, b1_ref, wg_ref, wu_ref, wd_ref, g2_ref, b2_ref, o_ref) = refs[n_parts:]
    if n_parts == 1:
        a = part_refs[0][...]
    else:
        a = jnp.concatenate([r[...] for r in part_refs], axis=1)
    mix = jnp.dot(a, wo_ref[...], preferred_element_type=_F32)
    x1 = _layer_norm(ALPHA * x_ref[...] + mix, g1_ref[...], b1_ref[...])
    xb = x1.astype(_BF16)
    hidden = []
    for c in range(N_FF_CHUNKS):
        gate = jnp.dot(xb, wg_ref[c], preferred_element_type=_F32)
        up = jnp.dot(xb, wu_ref[c], preferred_element_type=_F32)
        hidden.append((gate * jax.nn.sigmoid(gate) * up).astype(_BF16))
    ffn = jnp.dot(jnp.concatenate(hidden, axis=1), wd_ref[...], preferred_element_type=_F32)
    o_ref[...] = _layer_norm(ALPHA * x1 + ffn, g2_ref[...], b2_ref[...])


def _layer_tail(parts, x2d, w_out, g1, b1, wg, wu, wd, g2, b2):
    rows = x2d.shape[0]
    tm = ROW_TILE
    part_specs = [pl.BlockSpec((tm, p.shape[1]), lambda i: (i, 0)) for p in parts]
    consts = (w_out, g1, b1, wg, wu, wd, g2, b2)
    return pl.pallas_call(
        functools.partial(_layer_tail_kernel, n_parts=len(parts)),
        grid=(rows // tm,),
        in_specs=part_specs + [pl.BlockSpec((tm, D_MODEL), lambda i: (i, 0))] + [_const_spec(c.shape) for c in consts],
        out_specs=pl.BlockSpec((tm, D_MODEL), lambda i: (i, 0)),
        out_shape=jax.ShapeDtypeStruct((rows, D_MODEL), _F32),
        compiler_params=_params("parallel"),
        name="layer_tail",
    )(*parts, x2d, *consts)


def _rope_tables(angles):
    cos = jnp.cos(angles)
    sin = jnp.sin(angles)
    return jnp.concatenate([cos, cos], axis=1), jnp.concatenate([-sin, sin], axis=1)


def _axial_tables(seq_len):
    t = jnp.arange(seq_len)
    dim = HEAD_DIM // 2
    inv = ROPE_THETA ** (-jnp.arange(0, dim, 2, dtype=_F32) / dim)
    row_c, row_s = _rope_tables((t // GRID_W).astype(_F32)[:, None] * inv[None, :])
    col_c, col_s = _rope_tables((t % GRID_W).astype(_F32)[:, None] * inv[None, :])
    cos = jnp.concatenate([row_c, col_c] * (LANES // HEAD_DIM), axis=1)
    sin = jnp.concatenate([row_s, col_s] * (LANES // HEAD_DIM), axis=1)
    return cos, sin


def _seq_tables(seq_len):
    t = jnp.arange(seq_len)
    inv = ROPE_THETA ** (-jnp.arange(0, HEAD_DIM, 2, dtype=_F32) / HEAD_DIM)
    cos, sin = _rope_tables(t.astype(_F32)[:, None] * inv[None, :])
    reps = LANES // HEAD_DIM
    return jnp.concatenate([cos] * reps, axis=1), jnp.concatenate([sin] * reps, axis=1)


def _ffn_weights(wg, wu, wd):
    chunked = lambda w: jnp.transpose(w.reshape(D_MODEL, N_FF_CHUNKS, FF_CHUNK), (1, 0, 2)).astype(_BF16)
    return chunked(wg), chunked(wu), wd.astype(_BF16)


def _trunk(x, p):
    batch, seq_len, _ = x.shape
    x2d = x.reshape(batch * seq_len, D_MODEL)
    row = lambda v: v.reshape(1, -1)

    h0, na_vt, vt0 = _proj0(x2d, p["w_in0_a"], p["w_in0_vt"], p["w_in0_b"], p["ax_cos"][:seq_len],
                            p["ax_sin"][:seq_len], p["gqa_gain"], p["ones_bd"], seq_len)
    a_out = _na_attention(h0, na_vt, p["na_bias"], batch, seq_len).reshape(batch * seq_len, -1)
    b_out = _gqa_attention(h0, vt0, batch, seq_len).reshape(batch * seq_len, -1)
    x2d = _layer_tail([a_out, b_out], x2d, p["w_out0"], row(p["ln_mix_g"][0]), row(p["ln_mix_b"][0]),
                      *p["ffn0"], row(p["ln_ffn_g"][0]), row(p["ln_ffn_b"][0]))

    h1, vt1 = _proj1(x2d, p["w_in1_qk"], p["w_in1_vt"], p["seq_cos"][:seq_len], p["seq_sin"][:seq_len], seq_len)
    lambda_init = 0.8 - 0.6 * math.exp(-0.3 * 1)
    c_out = _diff_attention(h1, vt1, p["lq1"], p["lk1"], p["lq2"], p["lk2"], p["g_subln"],
                            lambda_init, batch, seq_len).reshape(batch * seq_len, -1)
    x2d = _layer_tail([c_out], x2d, p["w_out1"], row(p["ln_mix_g"][1]), row(p["ln_mix_b"][1]),
                      *p["ffn1"], row(p["ln_ffn_g"][1]), row(p["ln_ffn_b"][1]))
    return x2d.reshape(batch, seq_len, D_MODEL)


def kernel(x_prompt, x_sample, w_in_mix0, rpb_na, g_q_gqa, g_k_gqa, w_out_mix0, w_in_mix1, lam_q1, lam_k1,
           lam_q2, lam_k2, g_subln, w_out_mix1, ln_mix_g, ln_mix_b, w_ffn_gate, w_ffn_up, w_ffn_down,
           ln_ffn_g, ln_ffn_b):
    max_len = max(x_prompt.shape[1], x_sample.shape[1])
    ax_cos, ax_sin = _axial_tables(max_len)
    seq_cos, seq_sin = _seq_tables(max_len)
    head_of_lane = jnp.arange(LANES) // HEAD_DIM
    w0 = w_in_mix0[0].astype(_BF16)
    na_w = NA_HEADS * HEAD_DIM
    p = {
        "w_in0_a": w0[:, :2 * na_w],
        "w_in0_vt": w0[:, 2 * na_w:3 * na_w].T,
        "w_in0_b": w0[:, 3 * na_w:],
        "ax_cos": ax_cos, "ax_sin": ax_sin, "seq_cos": seq_cos, "seq_sin": seq_sin,
        "gqa_gain": jnp.concatenate([jnp.tile(g_q_gqa[0], GQA_HEADS), jnp.tile(g_k_gqa[0], GQA_KV_HEADS)]
                                    ).reshape(1, -1).astype(_F32),
        "ones_bd": (head_of_lane[:, None] == head_of_lane[None, :]).astype(_BF16),
        "na_bias": _na_bias_table(rpb_na[0]),
        "w_out0": w_out_mix0[0].astype(_BF16),
        "ffn0": _ffn_weights(w_ffn_gate[0], w_ffn_up[0], w_ffn_down[0]),
        "w_in1_qk": w_in_mix1[0][:, :2 * D_MODEL].astype(_BF16),
        "w_in1_vt": w_in_mix1[0][:, 2 * D_MODEL:].T.astype(_BF16),
        "lq1": lam_q1[0].reshape(1, -1), "lk1": lam_k1[0].reshape(1, -1),
        "lq2": lam_q2[0].reshape(1, -1), "lk2": lam_k2[0].reshape(1, -1),
        "g_subln": g_subln[0].reshape(1, -1),
        "w_out1": w_out_mix1[0].astype(_BF16),
        "ffn1": _ffn_weights(w_ffn_gate[1], w_ffn_up[1], w_ffn_down[1]),
        "ln_mix_g": ln_mix_g, "ln_mix_b": ln_mix_b, "ln_ffn_g": ln_ffn_g, "ln_ffn_b": ln_ffn_b,
    }
    return _trunk(x_prompt, p), _trunk(x_sample, p)
```

```python
import functools
import math

import jax
import jax.numpy as jnp
import numpy as np
from jax import lax
from jax.experimental import pallas as pl
from jax.experimental.pallas import tpu as pltpu

D_MODEL = 1024
DEPTH = 2
GRID_W = 64
HEAD_DIM = 64
NA_HEADS = 8
GQA_HEADS = 8
GQA_KV_HEADS = 2
DIFF_HEADS = 8
NA_WIN_H = 8
NA_WIN_W = 16
ROPE_THETA = 10000.0
D_FF = 2816
LN_EPS = 1e-5
RMS_EPS = 1e-6
SUBLN_EPS = 1e-5
ALPHA = (2.0 * DEPTH) ** 0.25
Q_SCALE = HEAD_DIM ** -0.5
LOG2E = math.log2(math.e)

LANES = 128
FF_CHUNK = 256
N_FF_CHUNKS = D_FF // FF_CHUNK
ROW_TILE = 512
ATT_TQ = 512
ATT_TK = 512
ONES_ROWS = 16
assert ATT_TK == ROW_TILE
MASK_VALUE = -1e30
VMEM_LIMIT = 56 * 1024 * 1024

_BF16 = jnp.bfloat16
_F32 = jnp.float32
_NT = (((1,), (1,)), ((), ()))


def _params(*semantics):
    return pltpu.CompilerParams(dimension_semantics=semantics, vmem_limit_bytes=VMEM_LIMIT)


def _const_spec(shape):
    zeros = (0,) * len(shape)
    return pl.BlockSpec(shape, lambda *_: zeros, pipeline_mode=pl.Buffered(1))


def _layer_norm(xf, gain, bias):
    mu = jnp.mean(xf, axis=-1, keepdims=True)
    xc = xf - mu
    var = jnp.mean(xc * xc, axis=-1, keepdims=True)
    return xc * lax.rsqrt(var + LN_EPS) * gain + bias


def _rotate_pairs(x, half):
    lane = lax.broadcasted_iota(jnp.int32, x.shape, 1)
    first = (lane % (2 * half)) < half
    return jnp.where(first, pltpu.roll(x, LANES - half, 1), pltpu.roll(x, half, 1))


H0_W = 2048
NA_VT_CHUNK = 256


def _proj0_kernel(x_ref, wa_ref, wvt_ref, wb_ref, cos_ref, sin_ref, gain_ref, ones_ref, o_ref, na_vt_ref, vt_ref):
    xb = x_ref[...].astype(_BF16)
    hb = jnp.dot(xb, wb_ref[...], preferred_element_type=_F32)
    cos = cos_ref[...]
    sin = sin_ref[...]
    ones_bd = ones_ref[...]
    lo64 = lax.broadcasted_iota(jnp.int32, cos.shape, 1) < HEAD_DIM

    def norm_rope(xg, gain):
        ssq = jnp.dot((xg * xg).astype(_BF16), ones_bd, preferred_element_type=_F32)
        n = xg * lax.rsqrt(ssq * (1.0 / HEAD_DIM) + RMS_EPS) * gain
        return n * cos + _rotate_pairs(n, 16) * sin

    for g in range(4):
        sl = slice(LANES * g, LANES * (g + 1))
        qg = norm_rope(hb[:, sl], gain_ref[:, sl]) * (Q_SCALE * LOG2E)
        o_ref[:, 1024 + LANES * g:1024 + LANES * (g + 1)] = qg.astype(_BF16)

    k = norm_rope(hb[:, 512:640], gain_ref[:, 512:640])
    kr = pltpu.roll(k, HEAD_DIM, 1)
    zero = jnp.zeros_like(k)
    parts = (jnp.where(lo64, k, zero), jnp.where(lo64, zero, kr),
             jnp.where(lo64, kr, zero), jnp.where(lo64, zero, k))
    for i, part in enumerate(parts):
        o_ref[:, 1536 + LANES * i:1536 + LANES * (i + 1)] = part.astype(_BF16)
    vt_ref[...] = hb[:, 640:768].T.astype(_BF16)

    ha = jnp.dot(xb, wa_ref[...], preferred_element_type=_F32)
    o_ref[:, 0:512] = (ha[:, 0:512] * (Q_SCALE * LOG2E)).astype(_BF16)
    o_ref[:, 512:1024] = ha[:, 512:1024].astype(_BF16)
    na_vt = lax.dot_general(wvt_ref[...], xb, _NT, preferred_element_type=_F32).astype(_BF16)
    for t in range(na_vt_ref.shape[0]):
        na_vt_ref[t] = na_vt[:, NA_VT_CHUNK * t:NA_VT_CHUNK * (t + 1)]


def _proj0(x2d, wa, wvt, wb, cos, sin, gain, ones_bd, seq_len):
    rows = x2d.shape[0]
    tm = ROW_TILE
    pos_blocks = seq_len // tm
    na_chunks = tm // NA_VT_CHUNK
    na_w = NA_HEADS * HEAD_DIM
    return pl.pallas_call(
        _proj0_kernel,
        grid=(rows // tm,),
        in_specs=[
            pl.BlockSpec((tm, D_MODEL), lambda i: (i, 0)),
            _const_spec(wa.shape),
            _const_spec(wvt.shape),
            _const_spec(wb.shape),
            pl.BlockSpec((tm, LANES), lambda i: (i % pos_blocks, 0)),
            pl.BlockSpec((tm, LANES), lambda i: (i % pos_blocks, 0)),
            _const_spec(gain.shape),
            _const_spec(ones_bd.shape),
        ],
        out_specs=[pl.BlockSpec((tm, H0_W), lambda i: (i, 0)),
                   pl.BlockSpec((na_chunks, na_w, NA_VT_CHUNK), lambda i: (i, 0, 0)),
                   pl.BlockSpec((None, LANES, tm), lambda i: (i, 0, 0))],
        out_shape=[jax.ShapeDtypeStruct((rows, H0_W), _BF16),
                   jax.ShapeDtypeStruct((rows // NA_VT_CHUNK, na_w, NA_VT_CHUNK), _BF16),
                   jax.ShapeDtypeStruct((rows // tm, LANES, tm), _BF16)],
        compiler_params=_params("parallel"),
        name="proj0",
    )(x2d, wa, wvt, wb, cos, sin, gain, ones_bd)


H1_W = 2 * D_MODEL


def _proj1_kernel(x_ref, w_ref, wvt_ref, cos_ref, sin_ref, o_ref, vt_ref):
    xb = x_ref[...].astype(_BF16)
    cos = cos_ref[...]
    sin = sin_ref[...]
    for part, scale in ((0, Q_SCALE * LOG2E), (1, 1.0)):
        base = part * D_MODEL
        h = jnp.dot(xb, w_ref[:, base:base + D_MODEL], preferred_element_type=_F32)
        for g in range(D_MODEL // LANES):
            xg = h[:, LANES * g:LANES * (g + 1)]
            r = xg * cos + _rotate_pairs(xg, 32) * sin
            if scale != 1.0:
                r = r * scale
            o_ref[:, base + LANES * g:base + LANES * (g + 1)] = r.astype(_BF16)
    vt_ref[...] = lax.dot_general(wvt_ref[...], xb, _NT, preferred_element_type=_F32).astype(_BF16)


def _proj1(x2d, w_qk, w_vt, cos, sin, seq_len):
    rows = x2d.shape[0]
    tm = ROW_TILE
    pos_blocks = seq_len // tm
    return pl.pallas_call(
        _proj1_kernel,
        grid=(rows // tm,),
        in_specs=[
            pl.BlockSpec((tm, D_MODEL), lambda i: (i, 0)),
            _const_spec(w_qk.shape),
            _const_spec(w_vt.shape),
            pl.BlockSpec((tm, LANES), lambda i: (i % pos_blocks, 0)),
            pl.BlockSpec((tm, LANES), lambda i: (i % pos_blocks, 0)),
        ],
        out_specs=[pl.BlockSpec((tm, H1_W), lambda i: (i, 0)),
                   pl.BlockSpec((None, D_MODEL, tm), lambda i: (i, 0, 0))],
        out_shape=[jax.ShapeDtypeStruct((rows, H1_W), _BF16),
                   jax.ShapeDtypeStruct((rows // tm, D_MODEL, tm), _BF16)],
        compiler_params=_params("parallel"),
        name="proj1",
    )(x2d, w_qk, w_vt, cos, sin)


def _with_ones(vt):
    return jnp.concatenate([vt, jnp.ones((ONES_ROWS, vt.shape[1]), vt.dtype)], axis=0)


def _issue_scores(s_ref, slot, c, s):
    s_ref[slot, c] = s
    return jnp.max(s, axis=0, keepdims=True)


def _attend_block(n_k, n_chain, scores_fn, next_scores_fn, vt_fn, s_ref, acc_ref, chunk_max):
    chunk_max = list(chunk_max)
    m_state = [None] * n_chain
    for i in range(n_k):
        slot = i % 2
        for c in range(n_chain):
            s_next = scores_fn(c, i + 1) if i + 1 < n_k else next_scores_fn(c)
            next_max = _issue_scores(s_ref, 1 - slot, c, s_next)
            m_next = chunk_max[c] if i == 0 else jnp.maximum(m_state[c], chunk_max[c])
            p = jnp.exp2(s_ref[slot, c] - m_next).astype(_BF16)
            pv = jnp.dot(_with_ones(vt_fn(c, i)), p, preferred_element_type=_F32)
            acc_ref[c] = pv if i == 0 else jnp.exp2(m_state[c] - m_next) * acc_ref[c] + pv
            m_state[c] = m_next
            chunk_max[c] = next_max
    return tuple(chunk_max)


def _for_each_query_block(n_q, tq, block, first_max):
    def body(qi, chunk_max):
        qs = pl.multiple_of(qi * tq, tq)
        qs_next = pl.multiple_of(jnp.minimum(qi + 1, n_q - 1) * tq, tq)
        return block(qs, qs_next, chunk_max)

    lax.fori_loop(0, n_q, body, first_max, unroll=2)


def _gqa_kernel(q_ref, k_even_ref, k_odd_ref, vt_ref, o_ref, s_ref, acc_ref, *, seq_len):
    tq, tk = ATT_TQ, ATT_TK
    n_q, n_k = seq_len // tq, seq_len // tk
    d = HEAD_DIM
    k_refs = (k_even_ref, k_odd_ref)

    def scores(qs, e, i):
        return lax.dot_general(k_refs[e][tk * i:tk * (i + 1), :], q_ref[pl.ds(qs, tq), :], _NT,
                               preferred_element_type=_F32)

    def block(qs, qs_next, chunk_max):
        chunk_max = _attend_block(n_k, 2, lambda e, i: scores(qs, e, i), lambda e: scores(qs_next, e, 0),
                                  lambda e, i: vt_ref[i], s_ref, acc_ref, chunk_max)
        halves = [acc_ref[e, 0:d, :] * (1.0 / acc_ref[e, d:d + 1, :]) for e in range(2)]
        out = jnp.concatenate(halves, axis=0).T
        o_ref[pl.ds(qs, tq), :] = out.astype(_BF16)
        return chunk_max

    first = tuple(_issue_scores(s_ref, 0, e, scores(0, e, 0)) for e in range(2))
    _for_each_query_block(n_q, tq, block, first)


def _gqa_attention(h0, vt0, batch, seq_len):
    h3 = h0.reshape(batch, seq_len, H0_W)
    n_pairs = GQA_HEADS // 2
    pairs_per_kv = n_pairs // GQA_KV_HEADS
    q_col = 2 * NA_HEADS * HEAD_DIM // LANES
    k_col = q_col + n_pairs
    return pl.pallas_call(
        functools.partial(_gqa_kernel, seq_len=seq_len),
        grid=(batch, n_pairs),
        in_specs=[
            pl.BlockSpec((None, seq_len, LANES), lambda b, p: (b, 0, q_col + p)),
            pl.BlockSpec((None, seq_len, LANES), lambda b, p: (b, 0, k_col + 2 * (p // pairs_per_kv))),
            pl.BlockSpec((None, seq_len, LANES), lambda b, p: (b, 0, k_col + 2 * (p // pairs_per_kv) + 1)),
            pl.BlockSpec((seq_len // ATT_TK, HEAD_DIM, ATT_TK), lambda b, p: (b, p // pairs_per_kv, 0)),
        ],
        out_specs=pl.BlockSpec((None, seq_len, LANES), lambda b, p: (b, 0, p)),
        out_shape=jax.ShapeDtypeStruct((batch, seq_len, GQA_HEADS * HEAD_DIM), _BF16),
        scratch_shapes=[pltpu.VMEM((2, 2, ATT_TK, ATT_TQ), _F32),
                        pltpu.VMEM((2, HEAD_DIM + ONES_ROWS, ATT_TQ), _F32)],
        compiler_params=_params("parallel", "parallel"),
        name="gqa_attention",
    )(h3, h3, h3, vt0)


NA_BLOCK_ROWS = 4
NA_SLAB_ROWS = NA_WIN_H + NA_BLOCK_ROWS
NA_Q = NA_BLOCK_ROWS * GRID_W
NA_KEYS = NA_SLAB_ROWS * GRID_W
assert NA_Q == NA_VT_CHUNK and NA_KEYS % NA_VT_CHUNK == 0


def _na_kernel(q_ref, k_ref, vt_ref, bias_ref, o_ref, s_ref, *, n_rows):
    n_blocks = n_rows // NA_BLOCK_ROWS
    lo64 = lax.broadcasted_iota(jnp.int32, (NA_KEYS, LANES), 1) < HEAD_DIM
    d = HEAD_DIM

    def slab_start(bi):
        return jnp.clip(bi - 1, 0, n_blocks - NA_SLAB_ROWS // NA_BLOCK_ROWS)

    def scores(bi, e):
        q = q_ref[pl.ds(pl.multiple_of(bi * NA_Q, NA_Q), NA_Q), :]
        kslab = k_ref[pl.ds(pl.multiple_of(slab_start(bi) * NA_Q, NA_Q), NA_KEYS), :]
        zero = jnp.zeros_like(kslab)
        kk = jnp.where(lo64, kslab, zero) if e == 0 else jnp.where(lo64, zero, kslab)
        return lax.dot_general(kk, q, _NT, preferred_element_type=_F32)

    def attend(bi, e, slot):
        layout = jnp.where(bi == 0, 0, jnp.where(bi == n_blocks - 1, 2, 1))
        s = s_ref[slot, e] + bias_ref[e, layout]
        p = jnp.exp2(s - jnp.max(s, axis=0, keepdims=True)).astype(_BF16)
        acc = None
        for t in range(NA_KEYS // NA_VT_CHUNK):
            vt = vt_ref[slab_start(bi) + t, d * e:d * (e + 1), :]
            part = jnp.dot(_with_ones(vt), p[NA_VT_CHUNK * t:NA_VT_CHUNK * (t + 1), :],
                           preferred_element_type=_F32)
            acc = part if acc is None else acc + part
        return acc[0:d, :] * (1.0 / acc[d:d + 1, :])

    def two_blocks(j, carry):
        b0 = 2 * j
        blocks = [b0, b0 + 1, jnp.minimum(b0 + 2, n_blocks - 1)]
        for u in range(2):
            halves = []
            for e in range(2):
                s_ref[1 - u, e] = scores(blocks[u + 1], e)
                halves.append(attend(blocks[u], e, u))
            qs = pl.multiple_of(blocks[u] * NA_Q, NA_Q)
            o_ref[pl.ds(qs, NA_Q), :] = jnp.concatenate(halves, axis=0).T.astype(_BF16)
        return carry

    for e in range(2):
        s_ref[0, e] = scores(0, e)
    lax.fori_loop(0, n_blocks // 2, two_blocks, 0)


def _na_attention(h0, na_vt, bias, batch, seq_len):
    h3 = h0.reshape(batch, seq_len, H0_W)
    n_pairs = NA_HEADS // 2
    chunks = seq_len // NA_VT_CHUNK
    return pl.pallas_call(
        functools.partial(_na_kernel, n_rows=seq_len // GRID_W),
        grid=(n_pairs, batch),
        in_specs=[
            pl.BlockSpec((None, seq_len, LANES), lambda g, b: (b, 0, g)),
            pl.BlockSpec((None, seq_len, LANES), lambda g, b: (b, 0, n_pairs + g)),
            pl.BlockSpec((chunks, LANES, NA_VT_CHUNK), lambda g, b: (b, g, 0)),
            pl.BlockSpec((2, 3, NA_KEYS, NA_Q), lambda g, b: (g, 0, 0, 0)),
        ],
        out_specs=pl.BlockSpec((None, seq_len, LANES), lambda g, b: (b, 0, g)),
        out_shape=jax.ShapeDtypeStruct((batch, seq_len, NA_HEADS * HEAD_DIM), _BF16),
        scratch_shapes=[pltpu.VMEM((2, 2, NA_KEYS, NA_Q), _F32)],
        compiler_params=_params("parallel", "parallel"),
        name="na_attention",
    )(h3, h3, na_vt, bias)


def _na_bias_table(rpb):
    cols = np.arange(GRID_W)
    col_start = np.clip(cols - NA_WIN_W // 2, 0, GRID_W - NA_WIN_W)
    kc = np.arange(GRID_W)
    in_cols = (kc[:, None] >= col_start[None, :]) & (kc[:, None] < col_start[None, :] + NA_WIN_W)
    col_idx = kc[:, None] - cols[None, :] + (NA_WIN_W - 1)
    i = np.arange(NA_BLOCK_ROWS)
    j = np.arange(NA_SLAB_ROWS)
    slab_off = np.array([0, -NA_WIN_H // 2, -NA_WIN_H])
    win_off = np.stack([np.zeros_like(i), i - NA_WIN_H // 2, np.full_like(i, -NA_WIN_H // 2)])
    key_row = slab_off[:, None] + j[None, :]
    in_rows = ((key_row[:, :, None] >= win_off[:, None, :])
               & (key_row[:, :, None] < win_off[:, None, :] + NA_WIN_H))
    row_idx = key_row[:, :, None] - i[None, None, :] + (NA_WIN_H - 1)
    row_pick = (in_rows[..., None] & (row_idx[..., None] == np.arange(2 * NA_WIN_H - 1))).astype(np.float32)
    col_pick = (in_cols[..., None] & (col_idx[..., None] == np.arange(2 * NA_WIN_W - 1))).astype(np.float32)
    by_col = jnp.einsum("hab,kcb->hakc", rpb, col_pick, precision=lax.Precision.HIGHEST)
    table = jnp.einsum("ljia,hakc->hljkic", row_pick, by_col, precision=lax.Precision.HIGHEST)
    valid = in_rows[:, :, None, :, None] & in_cols[None, None, :, None, :]
    table = jnp.where(valid[None], table * LOG2E, MASK_VALUE)
    return table.reshape(rpb.shape[0], 3, NA_KEYS, NA_Q).astype(_F32)


def _diff_kernel(q_ref, k_ref, vt_ref, lq1_ref, lk1_ref, lq2_ref, lk2_ref, g_ref, o_ref, s_ref, acc_ref,
                 *, seq_len, lambda_init):
    tq, tk = ATT_TQ, ATT_TK
    n_q, n_k = seq_len // tq, seq_len // tk
    d = 2 * HEAD_DIM
    lam = (jnp.exp(jnp.sum(lq1_ref[...] * lk1_ref[...], axis=1, keepdims=True))
           - jnp.exp(jnp.sum(lq2_ref[...] * lk2_ref[...], axis=1, keepdims=True)) + lambda_init)
    lo64 = lax.broadcasted_iota(jnp.int32, (tk, LANES), 1) < HEAD_DIM
    gain = g_ref[...] * (1.0 - lambda_init)

    def scores(qs, c, i):
        kblk = k_ref[tk * i:tk * (i + 1), :]
        zero = jnp.zeros_like(kblk)
        kk = jnp.where(lo64, kblk, zero) if c == 0 else jnp.where(lo64, zero, kblk)
        return lax.dot_general(kk, q_ref[pl.ds(qs, tq), :], _NT, preferred_element_type=_F32)

    def block(qs, qs_next, chunk_max):
        chunk_max = _attend_block(n_k, 2, lambda c, i: scores(qs, c, i), lambda c: scores(qs_next, c, 0),
                                  lambda c, i: vt_ref[i], s_ref, acc_ref, chunk_max)
        o1 = acc_ref[0, 0:d, :] * (1.0 / acc_ref[0, d:d + 1, :])
        o2 = acc_ref[1, 0:d, :] * (1.0 / acc_ref[1, d:d + 1, :])
        o = (o1 - lam * o2).T
        ms = jnp.mean(o * o, axis=1, keepdims=True)
        o_ref[pl.ds(qs, tq), :] = (o * lax.rsqrt(ms + SUBLN_EPS) * gain).astype(_BF16)
        return chunk_max

    first = tuple(_issue_scores(s_ref, 0, c, scores(0, c, 0)) for c in range(2))
    _for_each_query_block(n_q, tq, block, first)


def _diff_attention(h1, vt1, lq1, lk1, lq2, lk2, g_subln, lambda_init, batch, seq_len):
    h3 = h1.reshape(batch, seq_len, H1_W)
    heads = DIFF_HEADS
    small = lambda n: pl.BlockSpec((1, n), lambda b, h: (0, 0))
    return pl.pallas_call(
        functools.partial(_diff_kernel, seq_len=seq_len, lambda_init=lambda_init),
        grid=(batch, heads),
        in_specs=[
            pl.BlockSpec((None, seq_len, LANES), lambda b, h: (b, 0, h)),
            pl.BlockSpec((None, seq_len, LANES), lambda b, h: (b, 0, heads + h)),
            pl.BlockSpec((seq_len // ATT_TK, LANES, ATT_TK), lambda b, h: (b, h, 0)),
            small(HEAD_DIM), small(HEAD_DIM), small(HEAD_DIM), small(HEAD_DIM), small(2 * HEAD_DIM),
        ],
        out_specs=pl.BlockSpec((None, seq_len, LANES), lambda b, h: (b, 0, h)),
        out_shape=jax.ShapeDtypeStruct((batch, seq_len, D_MODEL), _BF16),
        scratch_shapes=[pltpu.VMEM((2, 2, ATT_TK, ATT_TQ), _F32),
                        pltpu.VMEM((2, 2 * HEAD_DIM + ONES_ROWS, ATT_TQ), _F32)],
        compiler_params=_params("parallel", "parallel"),
        name="diff_attention",
    )(h3, h3, vt1, lq1, lk1, lq2, lk2, g_subln)


def _layer_tail_kernel(*refs, n_parts):
    part_refs = refs[:n_parts]
    (x_ref, wo_ref, g1_ref, b1_ref, wg_ref, wu_ref, wd_ref, g2_ref, b2_ref, o_ref) = refs[n_parts:]
    if n_parts == 1:
        a = part_refs[0][...]
    else:
        a = jnp.concatenate([r[...] for r in part_refs], axis=1)
    mix = jnp.dot(a, wo_ref[...], preferred_element_type=_F32)
    x1 = _layer_norm(ALPHA * x_ref[...] + mix, g1_ref[...], b1_ref[...])
    xb = x1.astype(_BF16)
    hidden = []
    for c in range(N_FF_CHUNKS):
        gate = jnp.dot(xb, wg_ref[c], preferred_element_type=_F32)
        up = jnp.dot(xb, wu_ref[c], preferred_element_type=_F32)
        hidden.append((gate * jax.nn.sigmoid(gate) * up).astype(_BF16))
    ffn = jnp.dot(jnp.concatenate(hidden, axis=1), wd_ref[...], preferred_element_type=_F32)
    o_ref[...] = _layer_norm(ALPHA * x1 + ffn, g2_ref[...], b2_ref[...])


def _layer_tail(parts, x2d, w_out, g1, b1, wg, wu, wd, g2, b2):
    rows = x2d.shape[0]
    tm = ROW_TILE
    part_specs = [pl.BlockSpec((tm, p.shape[1]), lambda i: (i, 0)) for p in parts]
    consts = (w_out, g1, b1, wg, wu, wd, g2, b2)
    return pl.pallas_call(
        functools.partial(_layer_tail_kernel, n_parts=len(parts)),
        grid=(rows // tm,),
        in_specs=part_specs + [pl.BlockSpec((tm, D_MODEL), lambda i: (i, 0))] + [_const_spec(c.shape) for c in consts],
        out_specs=pl.BlockSpec((tm, D_MODEL), lambda i: (i, 0)),
        out_shape=jax.ShapeDtypeStruct((rows, D_MODEL), _F32),
        compiler_params=_params("parallel"),
        name="layer_tail",
    )(*parts, x2d, *consts)


def _rope_tables(angles):
    cos = jnp.cos(angles)
    sin = jnp.sin(angles)
    return jnp.concatenate([cos, cos], axis=1), jnp.concatenate([-sin, sin], axis=1)


def _axial_tables(seq_len):
    t = jnp.arange(seq_len)
    dim = HEAD_DIM // 2
    inv = ROPE_THETA ** (-jnp.arange(0, dim, 2, dtype=_F32) / dim)
    row_c, row_s = _rope_tables((t // GRID_W).astype(_F32)[:, None] * inv[None, :])
    col_c, col_s = _rope_tables((t % GRID_W).astype(_F32)[:, None] * inv[None, :])
    cos = jnp.concatenate([row_c, col_c] * (LANES // HEAD_DIM), axis=1)
    sin = jnp.concatenate([row_s, col_s] * (LANES // HEAD_DIM), axis=1)
    return cos, sin


def _seq_tables(seq_len):
    t = jnp.arange(seq_len)
    inv = ROPE_THETA ** (-jnp.arange(0, HEAD_DIM, 2, dtype=_F32) / HEAD_DIM)
    cos, sin = _rope_tables(t.astype(_F32)[:, None] * inv[None, :])
    reps = LANES // HEAD_DIM
    return jnp.concatenate([cos] * reps, axis=1), jnp.concatenate([sin] * reps, axis=1)


def _ffn_weights(wg, wu, wd):
    chunked = lambda w: jnp.transpose(w.reshape(D_MODEL, N_FF_CHUNKS, FF_CHUNK), (1, 0, 2)).astype(_BF16)
    return chunked(wg), chunked(wu), wd.astype(_BF16)


def _trunk(x, p):
    batch, seq_len, _ = x.shape
    x2d = x.reshape(batch * seq_len, D_MODEL)
    row = lambda v: v.reshape(1, -1)

    h0, na_vt, vt0 = _proj0(x2d, p["w_in0_a"], p["w_in0_vt"], p["w_in0_b"], p["ax_cos"][:seq_len],
                            p["ax_sin"][:seq_len], p["gqa_gain"], p["ones_bd"], seq_len)
    a_out = _na_attention(h0, na_vt, p["na_bias"], batch, seq_len).reshape(batch * seq_len, -1)
    b_out = _gqa_attention(h0, vt0, batch, seq_len).reshape(batch * seq_len, -1)
    x2d = _layer_tail([a_out, b_out], x2d, p["w_out0"], row(p["ln_mix_g"][0]), row(p["ln_mix_b"][0]),
                      *p["ffn0"], row(p["ln_ffn_g"][0]), row(p["ln_ffn_b"][0]))

    h1, vt1 = _proj1(x2d, p["w_in1_qk"], p["w_in1_vt"], p["seq_cos"][:seq_len], p["seq_sin"][:seq_len], seq_len)
    lambda_init = 0.8 - 0.6 * math.exp(-0.3 * 1)
    c_out = _diff_attention(h1, vt1, p["lq1"], p["lk1"], p["lq2"], p["lk2"], p["g_subln"],
                            lambda_init, batch, seq_len).reshape(batch * seq_len, -1)
    x2d = _layer_tail([c_out], x2d, p["w_out1"], row(p["ln_mix_g"][1]), row(p["ln_mix_b"][1]),
                      *p["ffn1"], row(p["ln_ffn_g"][1]), row(p["ln_ffn_b"][1]))
    return x2d.reshape(batch, seq_len, D_MODEL)


def kernel(x_prompt, x_sample, w_in_mix0, rpb_na, g_q_gqa, g_k_gqa, w_out_mix0, w_in_mix1, lam_q1, lam_k1,
           lam_q2, lam_k2, g_subln, w_out_mix1, ln_mix_g, ln_mix_b, w_ffn_gate, w_ffn_up, w_ffn_down,
           ln_ffn_g, ln_ffn_b):
    max_len = max(x_prompt.shape[1], x_sample.shape[1])
    ax_cos, ax_sin = _axial_tables(max_len)
    seq_cos, seq_sin = _seq_tables(max_len)
    head_of_lane = jnp.arange(LANES) // HEAD_DIM
    w0 = w_in_mix0[0].astype(_BF16)
    na_w = NA_HEADS * HEAD_DIM
    p = {
        "w_in0_a": w0[:, :2 * na_w],
        "w_in0_vt": w0[:, 2 * na_w:3 * na_w].T,
        "w_in0_b": w0[:, 3 * na_w:],
        "ax_cos": ax_cos, "ax_sin": ax_sin, "seq_cos": seq_cos, "seq_sin": seq_sin,
        "gqa_gain": jnp.concatenate([jnp.tile(g_q_gqa[0], GQA_HEADS), jnp.tile(g_k_gqa[0], GQA_KV_HEADS)]
                                    ).reshape(1, -1).astype(_F32),
        "ones_bd": (head_of_lane[:, None] == head_of_lane[None, :]).astype(_BF16),
        "na_bias": _na_bias_table(rpb_na[0]),
        "w_out0": w_out_mix0[0].astype(_BF16),
        "ffn0": _ffn_weights(w_ffn_gate[0], w_ffn_up[0], w_ffn_down[0]),
        "w_in1_qk": w_in_mix1[0][:, :2 * D_MODEL].astype(_BF16),
        "w_in1_vt": w_in_mix1[0][:, 2 * D_MODEL:].T.astype(_BF16),
        "lq1": lam_q1[0].reshape(1, -1), "lk1": lam_k1[0].reshape(1, -1),
        "lq2": lam_q2[0].reshape(1, -1), "lk2": lam_k2[0].reshape(1, -1),
        "g_subln": g_subln[0].reshape(1, -1),
        "w_out1": w_out_mix1[0].astype(_BF16),
        "ffn1": _ffn_weights(w_ffn_gate[1], w_ffn_up[1], w_ffn_down[1]),
        "ln_mix_g": ln_mix_g, "ln_mix_b": ln_mix_b, "ln_ffn_g": ln_ffn_g, "ln_ffn_b": ln_ffn_b,
    }
    return _trunk(x_prompt, p), _trunk(x_sample, p)
```

```python
import functools
import math

import jax
import jax.numpy as jnp
import numpy as np
from jax import lax
from jax.experimental import pallas as pl
from jax.experimental.pallas import tpu as pltpu

D_MODEL = 1024
DEPTH = 2
GRID_W = 64
HEAD_DIM = 64
NA_HEADS = 8
GQA_HEADS = 8
GQA_KV_HEADS = 2
DIFF_HEADS = 8
NA_WIN_H = 8
NA_WIN_W = 16
ROPE_THETA = 10000.0
D_FF = 2816
LN_EPS = 1e-5
RMS_EPS = 1e-6
SUBLN_EPS = 1e-5
ALPHA = (2.0 * DEPTH) ** 0.25
Q_SCALE = HEAD_DIM ** -0.5
LOG2E = math.log2(math.e)

LANES = 128
FF_CHUNK = 256
N_FF_CHUNKS = D_FF // FF_CHUNK
ROW_TILE = 512
TAIL_TILE = 1024
ATT_TQ = 512
ATT_TK = 512
ONES_ROWS = 16
assert ATT_TK == ROW_TILE
MASK_VALUE = -1e30
VMEM_LIMIT = 56 * 1024 * 1024

_BF16 = jnp.bfloat16
_F32 = jnp.float32
_NT = (((1,), (1,)), ((), ()))


def _params(*semantics):
    return pltpu.CompilerParams(dimension_semantics=semantics, vmem_limit_bytes=VMEM_LIMIT)


def _const_spec(shape):
    zeros = (0,) * len(shape)
    return pl.BlockSpec(shape, lambda *_: zeros, pipeline_mode=pl.Buffered(1))


def _layer_norm(xf, gain, bias):
    mu = jnp.mean(xf, axis=-1, keepdims=True)
    xc = xf - mu
    var = jnp.mean(xc * xc, axis=-1, keepdims=True)
    return xc * lax.rsqrt(var + LN_EPS) * gain + bias


def _rotate_pairs(x, half):
    lane = lax.broadcasted_iota(jnp.int32, x.shape, 1)
    first = (lane % (2 * half)) < half
    return jnp.where(first, pltpu.roll(x, LANES - half, 1), pltpu.roll(x, half, 1))


H0_W = 2048
NA_VT_CHUNK = 256


def _proj0_kernel(x_ref, wa_ref, wvt_ref, wb_ref, cos_ref, sin_ref, gain_ref, ones_ref, o_ref, na_vt_ref, vt_ref):
    xb = x_ref[...].astype(_BF16)
    hb = jnp.dot(xb, wb_ref[...], preferred_element_type=_F32)
    cos = cos_ref[...]
    sin = sin_ref[...]
    ones_bd = ones_ref[...]
    lo64 = lax.broadcasted_iota(jnp.int32, cos.shape, 1) < HEAD_DIM

    def norm_rope(xg, gain):
        ssq = jnp.dot((xg * xg).astype(_BF16), ones_bd, preferred_element_type=_F32)
        n = xg * lax.rsqrt(ssq * (1.0 / HEAD_DIM) + RMS_EPS) * gain
        return n * cos + _rotate_pairs(n, 16) * sin

    for g in range(4):
        sl = slice(LANES * g, LANES * (g + 1))
        qg = norm_rope(hb[:, sl], gain_ref[:, sl]) * (Q_SCALE * LOG2E)
        o_ref[:, 1024 + LANES * g:1024 + LANES * (g + 1)] = qg.astype(_BF16)

    k = norm_rope(hb[:, 512:640], gain_ref[:, 512:640])
    kr = pltpu.roll(k, HEAD_DIM, 1)
    zero = jnp.zeros_like(k)
    parts = (jnp.where(lo64, k, zero), jnp.where(lo64, zero, kr),
             jnp.where(lo64, kr, zero), jnp.where(lo64, zero, k))
    for i, part in enumerate(parts):
        o_ref[:, 1536 + LANES * i:1536 + LANES * (i + 1)] = part.astype(_BF16)
    vt_ref[...] = hb[:, 640:768].T.astype(_BF16)

    ha = jnp.dot(xb, wa_ref[...], preferred_element_type=_F32)
    o_ref[:, 0:512] = (ha[:, 0:512] * (Q_SCALE * LOG2E)).astype(_BF16)
    o_ref[:, 512:1024] = ha[:, 512:1024].astype(_BF16)
    na_vt = lax.dot_general(wvt_ref[...], xb, _NT, preferred_element_type=_F32).astype(_BF16)
    for t in range(na_vt_ref.shape[0]):
        na_vt_ref[t] = na_vt[:, NA_VT_CHUNK * t:NA_VT_CHUNK * (t + 1)]


def _proj0(x2d, wa, wvt, wb, cos, sin, gain, ones_bd, seq_len):
    rows = x2d.shape[0]
    tm = ROW_TILE
    pos_blocks = seq_len // tm
    na_chunks = tm // NA_VT_CHUNK
    na_w = NA_HEADS * HEAD_DIM
    return pl.pallas_call(
        _proj0_kernel,
        grid=(rows // tm,),
        in_specs=[
            pl.BlockSpec((tm, D_MODEL), lambda i: (i, 0)),
            _const_spec(wa.shape),
            _const_spec(wvt.shape),
            _const_spec(wb.shape),
            pl.BlockSpec((tm, LANES), lambda i: (i % pos_blocks, 0)),
            pl.BlockSpec((tm, LANES), lambda i: (i % pos_blocks, 0)),
            _const_spec(gain.shape),
            _const_spec(ones_bd.shape),
        ],
        out_specs=[pl.BlockSpec((tm, H0_W), lambda i: (i, 0)),
                   pl.BlockSpec((na_chunks, na_w, NA_VT_CHUNK), lambda i: (i, 0, 0)),
                   pl.BlockSpec((None, LANES, tm), lambda i: (i, 0, 0))],
        out_shape=[jax.ShapeDtypeStruct((rows, H0_W), _BF16),
                   jax.ShapeDtypeStruct((rows // NA_VT_CHUNK, na_w, NA_VT_CHUNK), _BF16),
                   jax.ShapeDtypeStruct((rows // tm, LANES, tm), _BF16)],
        compiler_params=_params("parallel"),
        name="proj0",
    )(x2d, wa, wvt, wb, cos, sin, gain, ones_bd)


H1_W = 2 * D_MODEL


def _proj1_kernel(x_ref, w_ref, wvt_ref, cos_ref, sin_ref, o_ref, vt_ref):
    xb = x_ref[...].astype(_BF16)
    cos = cos_ref[...]
    sin = sin_ref[...]
    for part, scale in ((0, Q_SCALE * LOG2E), (1, 1.0)):
        base = part * D_MODEL
        h = jnp.dot(xb, w_ref[:, base:base + D_MODEL], preferred_element_type=_F32)
        for g in range(D_MODEL // LANES):
            xg = h[:, LANES * g:LANES * (g + 1)]
            r = xg * cos + _rotate_pairs(xg, 32) * sin
            if scale != 1.0:
                r = r * scale
            o_ref[:, base + LANES * g:base + LANES * (g + 1)] = r.astype(_BF16)
    vt_ref[...] = lax.dot_general(wvt_ref[...], xb, _NT, preferred_element_type=_F32).astype(_BF16)


def _proj1(x2d, w_qk, w_vt, cos, sin, seq_len):
    rows = x2d.shape[0]
    tm = ROW_TILE
    pos_blocks = seq_len // tm
    return pl.pallas_call(
        _proj1_kernel,
        grid=(rows // tm,),
        in_specs=[
            pl.BlockSpec((tm, D_MODEL), lambda i: (i, 0)),
            _const_spec(w_qk.shape),
            _const_spec(w_vt.shape),
            pl.BlockSpec((tm, LANES), lambda i: (i % pos_blocks, 0)),
            pl.BlockSpec((tm, LANES), lambda i: (i % pos_blocks, 0)),
        ],
        out_specs=[pl.BlockSpec((tm, H1_W), lambda i: (i, 0)),
                   pl.BlockSpec((None, D_MODEL, tm), lambda i: (i, 0, 0))],
        out_shape=[jax.ShapeDtypeStruct((rows, H1_W), _BF16),
                   jax.ShapeDtypeStruct((rows // tm, D_MODEL, tm), _BF16)],
        compiler_params=_params("parallel"),
        name="proj1",
    )(x2d, w_qk, w_vt, cos, sin)


def _with_ones(vt):
    return jnp.concatenate([vt, jnp.ones((ONES_ROWS, vt.shape[1]), vt.dtype)], axis=0)


def _issue_scores(s_ref, slot, c, s):
    s_ref[slot, c] = s
    return jnp.max(s, axis=0, keepdims=True)


def _attend_block(n_k, n_chain, scores_fn, next_scores_fn, vt_fn, s_ref, acc_ref, chunk_max):
    chunk_max = list(chunk_max)
    m_state = [None] * n_chain
    for i in range(n_k):
        slot = i % 2
        for c in range(n_chain):
            s_next = scores_fn(c, i + 1) if i + 1 < n_k else next_scores_fn(c)
            next_max = _issue_scores(s_ref, 1 - slot, c, s_next)
            m_next = chunk_max[c] if i == 0 else jnp.maximum(m_state[c], chunk_max[c])
            p = jnp.exp2(s_ref[slot, c] - m_next).astype(_BF16)
            pv = jnp.dot(_with_ones(vt_fn(c, i)), p, preferred_element_type=_F32)
            acc_ref[c] = pv if i == 0 else jnp.exp2(m_state[c] - m_next) * acc_ref[c] + pv
            m_state[c] = m_next
            chunk_max[c] = next_max
    return tuple(chunk_max)


def _for_each_query_block(n_q, tq, block, first_max):
    def body(qi, chunk_max):
        qs = pl.multiple_of(qi * tq, tq)
        qs_next = pl.multiple_of(jnp.minimum(qi + 1, n_q - 1) * tq, tq)
        return block(qs, qs_next, chunk_max)

    lax.fori_loop(0, n_q, body, first_max, unroll=2)


def _gqa_kernel(q_ref, k_ref, vt_ref, o_ref, s_ref, acc_ref, *, seq_len):
    tq, tk = ATT_TQ, ATT_TK
    n_q, n_k = seq_len // tq, seq_len // tk
    n_pairs = GQA_HEADS // 2
    d = HEAD_DIM

    def scores(qs, pair, e, i):
        kvh = pair // (n_pairs // GQA_KV_HEADS)
        col = 2 * LANES * kvh + LANES * e
        q = q_ref[pl.ds(qs, tq), LANES * pair:LANES * (pair + 1)]
        return lax.dot_general(k_ref[tk * i:tk * (i + 1), col:col + LANES], q, _NT, preferred_element_type=_F32)

    def q_body(qi, chunk_max):
        qs = pl.multiple_of(qi * tq, tq)
        qs_next = pl.multiple_of(jnp.minimum(qi + 1, n_q - 1) * tq, tq)
        for pair in range(n_pairs):
            kvh = pair // (n_pairs // GQA_KV_HEADS)
            nxt = (qs, pair + 1) if pair + 1 < n_pairs else (qs_next, 0)
            chunk_max = _attend_block(
                n_k, 2,
                lambda e, i, pair=pair: scores(qs, pair, e, i),
                lambda e, nxt=nxt: scores(nxt[0], nxt[1], e, 0),
                lambda e, i, kvh=kvh: vt_ref[i, d * kvh:d * (kvh + 1), :],
                s_ref, acc_ref, chunk_max)
            halves = [acc_ref[e, 0:d, :] * (1.0 / acc_ref[e, d:d + 1, :]) for e in range(2)]
            out = jnp.concatenate(halves, axis=0).T
            o_ref[pl.ds(qs, tq), LANES * pair:LANES * (pair + 1)] = out.astype(_BF16)
        return chunk_max

    first = tuple(_issue_scores(s_ref, 0, e, scores(0, 0, e, 0)) for e in range(2))
    lax.fori_loop(0, n_q, q_body, first)


def _gqa_attention(h0, vt0, batch, seq_len):
    h3 = h0.reshape(batch, seq_len, H0_W)
    width = GQA_HEADS * HEAD_DIM
    return pl.pallas_call(
        functools.partial(_gqa_kernel, seq_len=seq_len),
        grid=(batch,),
        in_specs=[
            pl.BlockSpec((None, seq_len, width), lambda b: (b, 0, 2)),
            pl.BlockSpec((None, seq_len, width), lambda b: (b, 0, 3)),
            pl.BlockSpec((seq_len // ATT_TK, LANES, ATT_TK), lambda b: (b, 0, 0)),
        ],
        out_specs=pl.BlockSpec((None, seq_len, width), lambda b: (b, 0, 0)),
        out_shape=jax.ShapeDtypeStruct((batch, seq_len, width), _BF16),
        scratch_shapes=[pltpu.VMEM((2, 2, ATT_TK, ATT_TQ), _F32),
                        pltpu.VMEM((2, HEAD_DIM + ONES_ROWS, ATT_TQ), _F32)],
        compiler_params=_params("parallel"),
        name="gqa_attention",
    )(h3, h3, vt0)


NA_BLOCK_ROWS = 4
NA_SLAB_ROWS = NA_WIN_H + NA_BLOCK_ROWS
NA_Q = NA_BLOCK_ROWS * GRID_W
NA_KEYS = NA_SLAB_ROWS * GRID_W
assert NA_Q == NA_VT_CHUNK and NA_KEYS % NA_VT_CHUNK == 0


def _na_kernel(q_ref, k_ref, vt_ref, bias_ref, o_ref, s_ref, *, n_rows):
    n_blocks = n_rows // NA_BLOCK_ROWS
    lo64 = lax.broadcasted_iota(jnp.int32, (NA_KEYS, LANES), 1) < HEAD_DIM
    d = HEAD_DIM

    def slab_start(bi):
        return jnp.clip(bi - 1, 0, n_blocks - NA_SLAB_ROWS // NA_BLOCK_ROWS)

    def scores(bi, e):
        q = q_ref[pl.ds(pl.multiple_of(bi * NA_Q, NA_Q), NA_Q), :]
        kslab = k_ref[pl.ds(pl.multiple_of(slab_start(bi) * NA_Q, NA_Q), NA_KEYS), :]
        zero = jnp.zeros_like(kslab)
        kk = jnp.where(lo64, kslab, zero) if e == 0 else jnp.where(lo64, zero, kslab)
        return lax.dot_general(kk, q, _NT, preferred_element_type=_F32)

    def attend(bi, e, slot):
        layout = jnp.where(bi == 0, 0, jnp.where(bi == n_blocks - 1, 2, 1))
        s = s_ref[slot, e] + bias_ref[e, layout]
        p = jnp.exp2(s - jnp.max(s, axis=0, keepdims=True)).astype(_BF16)
        vt = jnp.concatenate([vt_ref[slab_start(bi) + t, d * e:d * (e + 1), :]
                              for t in range(NA_KEYS // NA_VT_CHUNK)], axis=1)
        acc = jnp.dot(_with_ones(vt), p, preferred_element_type=_F32)
        return acc[0:d, :] * (1.0 / acc[d:d + 1, :])

    def two_blocks(j, carry):
        b0 = 2 * j
        blocks = [b0, b0 + 1, jnp.minimum(b0 + 2, n_blocks - 1)]
        for u in range(2):
            halves = []
            for e in range(2):
                s_ref[1 - u, e] = scores(blocks[u + 1], e)
                halves.append(attend(blocks[u], e, u))
            qs = pl.multiple_of(blocks[u] * NA_Q, NA_Q)
            o_ref[pl.ds(qs, NA_Q), :] = jnp.concatenate(halves, axis=0).T.astype(_BF16)
        return carry

    for e in range(2):
        s_ref[0, e] = scores(0, e)
    lax.fori_loop(0, n_blocks // 2, two_blocks, 0)


def _na_attention(h0, na_vt, bias, batch, seq_len):
    h3 = h0.reshape(batch, seq_len, H0_W)
    n_pairs = NA_HEADS // 2
    chunks = seq_len // NA_VT_CHUNK
    return pl.pallas_call(
        functools.partial(_na_kernel, n_rows=seq_len // GRID_W),
        grid=(n_pairs, batch),
        in_specs=[
            pl.BlockSpec((None, seq_len, LANES), lambda g, b: (b, 0, g)),
            pl.BlockSpec((None, seq_len, LANES), lambda g, b: (b, 0, n_pairs + g)),
            pl.BlockSpec((chunks, LANES, NA_VT_CHUNK), lambda g, b: (b, g, 0)),
            pl.BlockSpec((2, 3, NA_KEYS, NA_Q), lambda g, b: (g, 0, 0, 0)),
        ],
        out_specs=pl.BlockSpec((None, seq_len, LANES), lambda g, b: (b, 0, g)),
        out_shape=jax.ShapeDtypeStruct((batch, seq_len, NA_HEADS * HEAD_DIM), _BF16),
        scratch_shapes=[pltpu.VMEM((2, 2, NA_KEYS, NA_Q), _F32)],
        compiler_params=_params("parallel", "parallel"),
        name="na_attention",
    )(h3, h3, na_vt, bias)


def _na_bias_table(rpb):
    cols = np.arange(GRID_W)
    col_start = np.clip(cols - NA_WIN_W // 2, 0, GRID_W - NA_WIN_W)
    kc = np.arange(GRID_W)
    in_cols = (kc[:, None] >= col_start[None, :]) & (kc[:, None] < col_start[None, :] + NA_WIN_W)
    col_idx = kc[:, None] - cols[None, :] + (NA_WIN_W - 1)
    i = np.arange(NA_BLOCK_ROWS)
    j = np.arange(NA_SLAB_ROWS)
    slab_off = np.array([0, -NA_WIN_H // 2, -NA_WIN_H])
    win_off = np.stack([np.zeros_like(i), i - NA_WIN_H // 2, np.full_like(i, -NA_WIN_H // 2)])
    key_row = slab_off[:, None] + j[None, :]
    in_rows = ((key_row[:, :, None] >= win_off[:, None, :])
               & (key_row[:, :, None] < win_off[:, None, :] + NA_WIN_H))
    row_idx = key_row[:, :, None] - i[None, None, :] + (NA_WIN_H - 1)
    row_pick = (in_rows[..., None] & (row_idx[..., None] == np.arange(2 * NA_WIN_H - 1))).astype(np.float32)
    col_pick = (in_cols[..., None] & (col_idx[..., None] == np.arange(2 * NA_WIN_W - 1))).astype(np.float32)
    by_col = jnp.einsum("hab,kcb->hakc", rpb, col_pick, precision=lax.Precision.HIGHEST)
    table = jnp.einsum("ljia,hakc->hljkic", row_pick, by_col, precision=lax.Precision.HIGHEST)
    valid = in_rows[:, :, None, :, None] & in_cols[None, None, :, None, :]
    table = jnp.where(valid[None], table * LOG2E, MASK_VALUE)
    return table.reshape(rpb.shape[0], 3, NA_KEYS, NA_Q).astype(_F32)


def _diff_kernel(q_ref, k_ref, vt_ref, lq1_ref, lk1_ref, lq2_ref, lk2_ref, g_ref, o_ref, s_ref, acc_ref,
                 *, seq_len, lambda_init):
    tq, tk = ATT_TQ, ATT_TK
    n_q, n_k = seq_len // tq, seq_len // tk
    d = 2 * HEAD_DIM
    lam = (jnp.exp(jnp.sum(lq1_ref[...] * lk1_ref[...], axis=1, keepdims=True))
           - jnp.exp(jnp.sum(lq2_ref[...] * lk2_ref[...], axis=1, keepdims=True)) + lambda_init)
    lo64 = lax.broadcasted_iota(jnp.int32, (tk, LANES), 1) < HEAD_DIM
    gain = g_ref[...] * (1.0 - lambda_init)

    def scores(qs, c, i):
        kblk = k_ref[tk * i:tk * (i + 1), :]
        zero = jnp.zeros_like(kblk)
        kk = jnp.where(lo64, kblk, zero) if c == 0 else jnp.where(lo64, zero, kblk)
        return lax.dot_general(kk, q_ref[pl.ds(qs, tq), :], _NT, preferred_element_type=_F32)

    def block(qs, qs_next, chunk_max):
        chunk_max = _attend_block(n_k, 2, lambda c, i: scores(qs, c, i), lambda c: scores(qs_next, c, 0),
                                  lambda c, i: vt_ref[i], s_ref, acc_ref, chunk_max)
        o1 = acc_ref[0, 0:d, :] * (1.0 / acc_ref[0, d:d + 1, :])
        o2 = acc_ref[1, 0:d, :] * (1.0 / acc_ref[1, d:d + 1, :])
        o = (o1 - lam * o2).T
        ms = jnp.mean(o * o, axis=1, keepdims=True)
        o_ref[pl.ds(qs, tq), :] = (o * lax.rsqrt(ms + SUBLN_EPS) * gain).astype(_BF16)
        return chunk_max

    first = tuple(_issue_scores(s_ref, 0, c, scores(0, c, 0)) for c in range(2))
    _for_each_query_block(n_q, tq, block, first)


def _diff_attention(h1, vt1, lq1, lk1, lq2, lk2, g_subln, lambda_init, batch, seq_len):
    h3 = h1.reshape(batch, seq_len, H1_W)
    heads = DIFF_HEADS
    small = lambda n: pl.BlockSpec((1, n), lambda b, h: (0, 0))
    return pl.pallas_call(
        functools.partial(_diff_kernel, seq_len=seq_len, lambda_init=lambda_init),
        grid=(batch, heads),
        in_specs=[
            pl.BlockSpec((None, seq_len, LANES), lambda b, h: (b, 0, h)),
            pl.BlockSpec((None, seq_len, LANES), lambda b, h: (b, 0, heads + h)),
            pl.BlockSpec((seq_len // ATT_TK, LANES, ATT_TK), lambda b, h: (b, h, 0)),
            small(HEAD_DIM), small(HEAD_DIM), small(HEAD_DIM), small(HEAD_DIM), small(2 * HEAD_DIM),
        ],
        out_specs=pl.BlockSpec((None, seq_len, LANES), lambda b, h: (b, 0, h)),
        out_shape=jax.ShapeDtypeStruct((batch, seq_len, D_MODEL), _BF16),
        scratch_shapes=[pltpu.VMEM((2, 2, ATT_TK, ATT_TQ), _F32),
                        pltpu.VMEM((2, 2 * HEAD_DIM + ONES_ROWS, ATT_TQ), _F32)],
        compiler_params=_params("parallel", "parallel"),
        name="diff_attention",
    )(h3, h3, vt1, lq1, lk1, lq2, lk2, g_subln)


def _layer_tail_kernel(*refs, n_parts):
    part_refs = refs[:n_parts]
    (x_ref, wo_ref, g1_ref, b1_ref, wg_ref, wu_ref, wd_ref, g2_ref, b2_ref, o_ref) = refs[n_parts:]
    for r in range(x_ref.shape[0] // ROW_TILE):
        rows = slice(ROW_TILE * r, ROW_TILE * (r + 1))
        if n_parts == 1:
            a = part_refs[0][rows, :]
        else:
            a = jnp.concatenate([p[rows, :] for p in part_refs], axis=1)
        mix = jnp.dot(a, wo_ref[...], preferred_element_type=_F32)
        x1 = _layer_norm(ALPHA * x_ref[rows, :] + mix, g1_ref[...], b1_ref[...])
        xb = x1.astype(_BF16)
        hidden = []
        for c in range(N_FF_CHUNKS):
            cols = slice(FF_CHUNK * c, FF_CHUNK * (c + 1))
            gate = jnp.dot(xb, wg_ref[:, cols], preferred_element_type=_F32)
            up = jnp.dot(xb, wu_ref[:, cols], preferred_element_type=_F32)
            hidden.append((gate * jax.nn.sigmoid(gate) * up).astype(_BF16))
        ffn = jnp.dot(jnp.concatenate(hidden, axis=1), wd_ref[...], preferred_element_type=_F32)
        o_ref[rows, :] = _layer_norm(ALPHA * x1 + ffn, g2_ref[...], b2_ref[...])


def _layer_tail(parts, x2d, w_out, g1, b1, wg, wu, wd, g2, b2):
    rows = x2d.shape[0]
    tm = TAIL_TILE
    part_specs = [pl.BlockSpec((tm, p.shape[1]), lambda i: (i, 0)) for p in parts]
    consts = (w_out, g1, b1, wg, wu, wd, g2, b2)
    return pl.pallas_call(
        functools.partial(_layer_tail_kernel, n_parts=len(parts)),
        grid=(rows // tm,),
        in_specs=part_specs + [pl.BlockSpec((tm, D_MODEL), lambda i: (i, 0))] + [_const_spec(c.shape) for c in consts],
        out_specs=pl.BlockSpec((tm, D_MODEL), lambda i: (i, 0)),
        out_shape=jax.ShapeDtypeStruct((rows, D_MODEL), _F32),
        compiler_params=_params("parallel"),
        name="layer_tail",
    )(*parts, x2d, *consts)


def _rope_tables(angles):
    cos = np.cos(angles)
    sin = np.sin(angles)
    return np.concatenate([cos, cos], axis=1), np.concatenate([-sin, sin], axis=1)


def _inv_freq(dim):
    return ROPE_THETA ** (-np.arange(0, dim, 2, dtype=np.float64) / dim)


def _axial_tables(seq_len):
    t = np.arange(seq_len)
    inv = _inv_freq(HEAD_DIM // 2)
    row_c, row_s = _rope_tables((t // GRID_W)[:, None] * inv[None, :])
    col_c, col_s = _rope_tables((t % GRID_W)[:, None] * inv[None, :])
    cos = np.concatenate([row_c, col_c] * (LANES // HEAD_DIM), axis=1)
    sin = np.concatenate([row_s, col_s] * (LANES // HEAD_DIM), axis=1)
    return cos.astype(np.float32), sin.astype(np.float32)


def _seq_tables(seq_len):
    t = np.arange(seq_len)
    cos, sin = _rope_tables(t[:, None] * _inv_freq(HEAD_DIM)[None, :])
    reps = LANES // HEAD_DIM
    return (np.concatenate([cos] * reps, axis=1).astype(np.float32),
            np.concatenate([sin] * reps, axis=1).astype(np.float32))


def _ffn_weights(wg, wu, wd):
    return wg.astype(_BF16), wu.astype(_BF16), wd.astype(_BF16)


def _trunk(x, p):
    batch, seq_len, _ = x.shape
    x2d = x.reshape(batch * seq_len, D_MODEL)
    row = lambda v: v.reshape(1, -1)

    h0, na_vt, vt0 = _proj0(x2d, p["w_in0_a"], p["w_in0_vt"], p["w_in0_b"], p["ax_cos"][:seq_len],
                            p["ax_sin"][:seq_len], p["gqa_gain"], p["ones_bd"], seq_len)
    a_out = _na_attention(h0, na_vt, p["na_bias"], batch, seq_len).reshape(batch * seq_len, -1)
    b_out = _gqa_attention(h0, vt0, batch, seq_len).reshape(batch * seq_len, -1)
    x2d = _layer_tail([a_out, b_out], x2d, p["w_out0"], row(p["ln_mix_g"][0]), row(p["ln_mix_b"][0]),
                      *p["ffn0"], row(p["ln_ffn_g"][0]), row(p["ln_ffn_b"][0]))

    h1, vt1 = _proj1(x2d, p["w_in1_qk"], p["w_in1_vt"], p["seq_cos"][:seq_len], p["seq_sin"][:seq_len], seq_len)
    lambda_init = 0.8 - 0.6 * math.exp(-0.3 * 1)
    c_out = _diff_attention(h1, vt1, p["lq1"], p["lk1"], p["lq2"], p["lk2"], p["g_subln"],
                            lambda_init, batch, seq_len).reshape(batch * seq_len, -1)
    x2d = _layer_tail([c_out], x2d, p["w_out1"], row(p["ln_mix_g"][1]), row(p["ln_mix_b"][1]),
                      *p["ffn1"], row(p["ln_ffn_g"][1]), row(p["ln_ffn_b"][1]))
    return x2d.reshape(batch, seq_len, D_MODEL)


def kernel(x_prompt, x_sample, w_in_mix0, rpb_na, g_q_gqa, g_k_gqa, w_out_mix0, w_in_mix1, lam_q1, lam_k1,
           lam_q2, lam_k2, g_subln, w_out_mix1, ln_mix_g, ln_mix_b, w_ffn_gate, w_ffn_up, w_ffn_down,
           ln_ffn_g, ln_ffn_b):
    max_len = max(x_prompt.shape[1], x_sample.shape[1])
    ax_cos, ax_sin = _axial_tables(max_len)
    seq_cos, seq_sin = _seq_tables(max_len)
    head_of_lane = jnp.arange(LANES) // HEAD_DIM
    w0 = w_in_mix0[0].astype(_BF16)
    na_w = NA_HEADS * HEAD_DIM
    p = {
        "w_in0_a": w0[:, :2 * na_w],
        "w_in0_vt": w0[:, 2 * na_w:3 * na_w].T,
        "w_in0_b": w0[:, 3 * na_w:],
        "ax_cos": ax_cos, "ax_sin": ax_sin, "seq_cos": seq_cos, "seq_sin": seq_sin,
        "gqa_gain": jnp.concatenate([jnp.tile(g_q_gqa[0], GQA_HEADS), jnp.tile(g_k_gqa[0], GQA_KV_HEADS)]
                                    ).reshape(1, -1).astype(_F32),
        "ones_bd": (head_of_lane[:, None] == head_of_lane[None, :]).astype(_BF16),
        "na_bias": _na_bias_table(rpb_na[0]),
        "w_out0": w_out_mix0[0].astype(_BF16),
        "ffn0": _ffn_weights(w_ffn_gate[0], w_ffn_up[0], w_ffn_down[0]),
        "w_in1_qk": w_in_mix1[0][:, :2 * D_MODEL].astype(_BF16),
        "w_in1_vt": w_in_mix1[0][:, 2 * D_MODEL:].T.astype(_BF16),
        "lq1": lam_q1[0].reshape(1, -1), "lk1": lam_k1[0].reshape(1, -1),
        "lq2": lam_q2[0].reshape(1, -1), "lk2": lam_k2[0].reshape(1, -1),
        "g_subln": g_subln[0].reshape(1, -1),
        "w_out1": w_out_mix1[0].astype(_BF16),
        "ffn1": _ffn_weights(w_ffn_gate[1], w_ffn_up[1], w_ffn_down[1]),
        "ln_mix_g": ln_mix_g, "ln_mix_b": ln_mix_b, "ln_ffn_g": ln_ffn_g, "ln_ffn_b": ln_ffn_b,
    }
    return _trunk(x_prompt, p), _trunk(x_sample, p)
```

```python
import functools
import math

import jax
import jax.numpy as jnp
import numpy as np
from jax import lax
from jax.experimental import pallas as pl
from jax.experimental.pallas import tpu as pltpu

D_MODEL = 1024
DEPTH = 2
GRID_W = 64
HEAD_DIM = 64
NA_HEADS = 8
GQA_HEADS = 8
GQA_KV_HEADS = 2
DIFF_HEADS = 8
NA_WIN_H = 8
NA_WIN_W = 16
ROPE_THETA = 10000.0
D_FF = 2816
LN_EPS = 1e-5
RMS_EPS = 1e-6
SUBLN_EPS = 1e-5
ALPHA = (2.0 * DEPTH) ** 0.25
Q_SCALE = HEAD_DIM ** -0.5
LOG2E = math.log2(math.e)

LANES = 128
FF_CHUNK = 256
N_FF_CHUNKS = D_FF // FF_CHUNK
ROW_TILE = 512
TAIL_TILE = 1024
ATT_TQ = 512
ATT_TK = 512
ONES_ROWS = 16
assert ATT_TK == ROW_TILE
MASK_VALUE = -1e30
VMEM_LIMIT = 56 * 1024 * 1024

_BF16 = jnp.bfloat16
_F32 = jnp.float32
_NT = (((1,), (1,)), ((), ()))


def _params(*semantics):
    return pltpu.CompilerParams(dimension_semantics=semantics, vmem_limit_bytes=VMEM_LIMIT)


def _const_spec(shape):
    zeros = (0,) * len(shape)
    return pl.BlockSpec(shape, lambda *_: zeros, pipeline_mode=pl.Buffered(1))


def _layer_norm(xf, gain, bias):
    mu = jnp.mean(xf, axis=-1, keepdims=True)
    xc = xf - mu
    var = jnp.mean(xc * xc, axis=-1, keepdims=True)
    return xc * lax.rsqrt(var + LN_EPS) * gain + bias


def _rotate_pairs(x, half):
    lane = lax.broadcasted_iota(jnp.int32, x.shape, 1)
    first = (lane % (2 * half)) < half
    return jnp.where(first, pltpu.roll(x, LANES - half, 1), pltpu.roll(x, half, 1))


H0_W = 2048
NA_VT_CHUNK = 256


def _proj0_kernel(x_ref, wa_ref, wvt_ref, wb_ref, cos_ref, sin_ref, gain_ref, ones_ref, o_ref, na_vt_ref, vt_ref):
    xb = x_ref[...].astype(_BF16)
    hb = jnp.dot(xb, wb_ref[...], preferred_element_type=_F32)
    cos = cos_ref[...]
    sin = sin_ref[...]
    ones_bd = ones_ref[...]
    lo64 = lax.broadcasted_iota(jnp.int32, cos.shape, 1) < HEAD_DIM

    def norm_rope(xg, gain):
        ssq = jnp.dot((xg * xg).astype(_BF16), ones_bd, preferred_element_type=_F32)
        n = xg * lax.rsqrt(ssq * (1.0 / HEAD_DIM) + RMS_EPS) * gain
        return n * cos + _rotate_pairs(n, 16) * sin

    for g in range(4):
        sl = slice(LANES * g, LANES * (g + 1))
        qg = norm_rope(hb[:, sl], gain_ref[:, sl]) * (Q_SCALE * LOG2E)
        o_ref[:, 1024 + LANES * g:1024 + LANES * (g + 1)] = qg.astype(_BF16)

    k = norm_rope(hb[:, 512:640], gain_ref[:, 512:640])
    kr = pltpu.roll(k, HEAD_DIM, 1)
    zero = jnp.zeros_like(k)
    parts = (jnp.where(lo64, k, zero), jnp.where(lo64, zero, kr),
             jnp.where(lo64, kr, zero), jnp.where(lo64, zero, k))
    for i, part in enumerate(parts):
        o_ref[:, 1536 + LANES * i:1536 + LANES * (i + 1)] = part.astype(_BF16)
    vt_ref[...] = hb[:, 640:768].T.astype(_BF16)

    ha = jnp.dot(xb, wa_ref[...], preferred_element_type=_F32)
    o_ref[:, 0:512] = (ha[:, 0:512] * (Q_SCALE * LOG2E)).astype(_BF16)
    o_ref[:, 512:1024] = ha[:, 512:1024].astype(_BF16)
    na_vt = lax.dot_general(wvt_ref[...], xb, _NT, preferred_element_type=_F32).astype(_BF16)
    for t in range(na_vt_ref.shape[0]):
        na_vt_ref[t] = na_vt[:, NA_VT_CHUNK * t:NA_VT_CHUNK * (t + 1)]


def _proj0(x2d, wa, wvt, wb, cos, sin, gain, ones_bd, seq_len):
    rows = x2d.shape[0]
    tm = ROW_TILE
    pos_blocks = seq_len // tm
    na_chunks = tm // NA_VT_CHUNK
    na_w = NA_HEADS * HEAD_DIM
    return pl.pallas_call(
        _proj0_kernel,
        grid=(rows // tm,),
        in_specs=[
            pl.BlockSpec((tm, D_MODEL), lambda i: (i, 0)),
            _const_spec(wa.shape),
            _const_spec(wvt.shape),
            _const_spec(wb.shape),
            pl.BlockSpec((tm, LANES), lambda i: (i % pos_blocks, 0)),
            pl.BlockSpec((tm, LANES), lambda i: (i % pos_blocks, 0)),
            _const_spec(gain.shape),
            _const_spec(ones_bd.shape),
        ],
        out_specs=[pl.BlockSpec((tm, H0_W), lambda i: (i, 0)),
                   pl.BlockSpec((na_chunks, na_w, NA_VT_CHUNK), lambda i: (i, 0, 0)),
                   pl.BlockSpec((None, LANES, tm), lambda i: (i, 0, 0))],
        out_shape=[jax.ShapeDtypeStruct((rows, H0_W), _BF16),
                   jax.ShapeDtypeStruct((rows // NA_VT_CHUNK, na_w, NA_VT_CHUNK), _BF16),
                   jax.ShapeDtypeStruct((rows // tm, LANES, tm), _BF16)],
        compiler_params=_params("parallel"),
        name="proj0",
    )(x2d, wa, wvt, wb, cos, sin, gain, ones_bd)


H1_W = 2 * D_MODEL


def _proj1_kernel(x_ref, w_ref, wvt_ref, cos_ref, sin_ref, o_ref, vt_ref):
    xb = x_ref[...].astype(_BF16)
    cos = cos_ref[...]
    sin = sin_ref[...]
    for part, scale in ((0, Q_SCALE * LOG2E), (1, 1.0)):
        base = part * D_MODEL
        h = jnp.dot(xb, w_ref[:, base:base + D_MODEL], preferred_element_type=_F32)
        for g in range(D_MODEL // LANES):
            xg = h[:, LANES * g:LANES * (g + 1)]
            r = xg * cos + _rotate_pairs(xg, 32) * sin
            if scale != 1.0:
                r = r * scale
            o_ref[:, base + LANES * g:base + LANES * (g + 1)] = r.astype(_BF16)
    vt_ref[...] = lax.dot_general(wvt_ref[...], xb, _NT, preferred_element_type=_F32).astype(_BF16)


def _proj1(x2d, w_qk, w_vt, cos, sin, seq_len):
    rows = x2d.shape[0]
    tm = ROW_TILE
    pos_blocks = seq_len // tm
    return pl.pallas_call(
        _proj1_kernel,
        grid=(rows // tm,),
        in_specs=[
            pl.BlockSpec((tm, D_MODEL), lambda i: (i, 0)),
            _const_spec(w_qk.shape),
            _const_spec(w_vt.shape),
            pl.BlockSpec((tm, LANES), lambda i: (i % pos_blocks, 0)),
            pl.BlockSpec((tm, LANES), lambda i: (i % pos_blocks, 0)),
        ],
        out_specs=[pl.BlockSpec((tm, H1_W), lambda i: (i, 0)),
                   pl.BlockSpec((None, D_MODEL, tm), lambda i: (i, 0, 0))],
        out_shape=[jax.ShapeDtypeStruct((rows, H1_W), _BF16),
                   jax.ShapeDtypeStruct((rows // tm, D_MODEL, tm), _BF16)],
        compiler_params=_params("parallel"),
        name="proj1",
    )(x2d, w_qk, w_vt, cos, sin)


def _with_ones(vt):
    return jnp.concatenate([vt, jnp.ones((ONES_ROWS, vt.shape[1]), vt.dtype)], axis=0)


def _issue_scores(s_ref, slot, c, s):
    s_ref[slot, c] = s
    return jnp.max(s, axis=0, keepdims=True)


def _attend_block(n_k, n_chain, scores_fn, next_scores_fn, vt_fn, s_ref, acc_ref, chunk_max):
    chunk_max = list(chunk_max)
    m_state = [None] * n_chain
    for i in range(n_k):
        slot = i % 2
        for c in range(n_chain):
            s_next = scores_fn(c, i + 1) if i + 1 < n_k else next_scores_fn(c)
            next_max = _issue_scores(s_ref, 1 - slot, c, s_next)
            m_next = chunk_max[c] if i == 0 else jnp.maximum(m_state[c], chunk_max[c])
            p = jnp.exp2(s_ref[slot, c] - m_next).astype(_BF16)
            pv = jnp.dot(_with_ones(vt_fn(c, i)), p, preferred_element_type=_F32)
            acc_ref[c] = pv if i == 0 else jnp.exp2(m_state[c] - m_next) * acc_ref[c] + pv
            m_state[c] = m_next
            chunk_max[c] = next_max
    return tuple(chunk_max)


def _for_each_query_block(n_q, tq, block, first_max):
    def body(qi, chunk_max):
        qs = pl.multiple_of(qi * tq, tq)
        qs_next = pl.multiple_of(jnp.minimum(qi + 1, n_q - 1) * tq, tq)
        return block(qs, qs_next, chunk_max)

    lax.fori_loop(0, n_q, body, first_max, unroll=2)


def _gqa_kernel(q_ref, k_ref, vt_ref, o_ref, s_ref, acc_ref, *, seq_len):
    tq, tk = ATT_TQ, ATT_TK
    n_q, n_k = seq_len // tq, seq_len // tk
    n_pairs = GQA_HEADS // 2
    d = HEAD_DIM

    def scores(qs, pair, e, i):
        kvh = pair // (n_pairs // GQA_KV_HEADS)
        col = 2 * LANES * kvh + LANES * e
        q = q_ref[pl.ds(qs, tq), LANES * pair:LANES * (pair + 1)]
        return lax.dot_general(k_ref[tk * i:tk * (i + 1), col:col + LANES], q, _NT, preferred_element_type=_F32)

    def q_body(qi, chunk_max):
        qs = pl.multiple_of(qi * tq, tq)
        qs_next = pl.multiple_of(jnp.minimum(qi + 1, n_q - 1) * tq, tq)
        for pair in range(n_pairs):
            kvh = pair // (n_pairs // GQA_KV_HEADS)
            nxt = (qs, pair + 1) if pair + 1 < n_pairs else (qs_next, 0)
            chunk_max = _attend_block(
                n_k, 2,
                lambda e, i, pair=pair: scores(qs, pair, e, i),
                lambda e, nxt=nxt: scores(nxt[0], nxt[1], e, 0),
                lambda e, i: vt_ref[i],
                s_ref, acc_ref, chunk_max)
            halves = [acc_ref[e, d * kvh:d * (kvh + 1), :] * (1.0 / acc_ref[e, 2 * d:2 * d + 1, :])
                      for e in range(2)]
            out = jnp.concatenate(halves, axis=0).T
            o_ref[pl.ds(qs, tq), LANES * pair:LANES * (pair + 1)] = out.astype(_BF16)
        return chunk_max

    first = tuple(_issue_scores(s_ref, 0, e, scores(0, 0, e, 0)) for e in range(2))
    lax.fori_loop(0, n_q, q_body, first)


def _gqa_attention(h0, vt0, batch, seq_len):
    h3 = h0.reshape(batch, seq_len, H0_W)
    width = GQA_HEADS * HEAD_DIM
    return pl.pallas_call(
        functools.partial(_gqa_kernel, seq_len=seq_len),
        grid=(batch,),
        in_specs=[
            pl.BlockSpec((None, seq_len, width), lambda b: (b, 0, 2)),
            pl.BlockSpec((None, seq_len, width), lambda b: (b, 0, 3)),
            pl.BlockSpec((seq_len // ATT_TK, LANES, ATT_TK), lambda b: (b, 0, 0)),
        ],
        out_specs=pl.BlockSpec((None, seq_len, width), lambda b: (b, 0, 0)),
        out_shape=jax.ShapeDtypeStruct((batch, seq_len, width), _BF16),
        scratch_shapes=[pltpu.VMEM((2, 2, ATT_TK, ATT_TQ), _F32),
                        pltpu.VMEM((2, GQA_KV_HEADS * HEAD_DIM + ONES_ROWS, ATT_TQ), _F32)],
        compiler_params=_params("parallel"),
        name="gqa_attention",
    )(h3, h3, vt0)


NA_BLOCK_ROWS = 4
NA_SLAB_ROWS = NA_WIN_H + NA_BLOCK_ROWS
NA_Q = NA_BLOCK_ROWS * GRID_W
NA_KEYS = NA_SLAB_ROWS * GRID_W
assert NA_Q == NA_VT_CHUNK and NA_KEYS % NA_VT_CHUNK == 0


def _na_kernel(q_ref, k_ref, vt_ref, bias_ref, o_ref, s_ref, *, n_rows):
    n_blocks = n_rows // NA_BLOCK_ROWS
    lo64 = lax.broadcasted_iota(jnp.int32, (NA_KEYS, LANES), 1) < HEAD_DIM
    d = HEAD_DIM

    def slab_start(bi):
        return jnp.clip(bi - 1, 0, n_blocks - NA_SLAB_ROWS // NA_BLOCK_ROWS)

    def scores(bi, e):
        q = q_ref[pl.ds(pl.multiple_of(bi * NA_Q, NA_Q), NA_Q), :]
        kslab = k_ref[pl.ds(pl.multiple_of(slab_start(bi) * NA_Q, NA_Q), NA_KEYS), :]
        zero = jnp.zeros_like(kslab)
        kk = jnp.where(lo64, kslab, zero) if e == 0 else jnp.where(lo64, zero, kslab)
        return lax.dot_general(kk, q, _NT, preferred_element_type=_F32)

    def attend(bi, e, slot):
        layout = jnp.where(bi == 0, 0, jnp.where(bi == n_blocks - 1, 2, 1))
        s = s_ref[slot, e] + bias_ref[e, layout]
        p = jnp.exp2(s - jnp.max(s, axis=0, keepdims=True)).astype(_BF16)
        vt = jnp.concatenate([vt_ref[slab_start(bi) + t] for t in range(NA_KEYS // NA_VT_CHUNK)],
                             axis=1)
        acc = jnp.dot(_with_ones(vt), p, preferred_element_type=_F32)
        return acc[d * e:d * (e + 1), :] * (1.0 / acc[2 * d:2 * d + 1, :])

    def two_blocks(j, carry):
        b0 = 2 * j
        blocks = [b0, b0 + 1, jnp.minimum(b0 + 2, n_blocks - 1)]
        for u in range(2):
            halves = []
            for e in range(2):
                s_ref[1 - u, e] = scores(blocks[u + 1], e)
                halves.append(attend(blocks[u], e, u))
            qs = pl.multiple_of(blocks[u] * NA_Q, NA_Q)
            o_ref[pl.ds(qs, NA_Q), :] = jnp.concatenate(halves, axis=0).T.astype(_BF16)
        return carry

    for e in range(2):
        s_ref[0, e] = scores(0, e)
    lax.fori_loop(0, n_blocks // 2, two_blocks, 0)


def _na_attention(h0, na_vt, bias, batch, seq_len):
    h3 = h0.reshape(batch, seq_len, H0_W)
    n_pairs = NA_HEADS // 2
    chunks = seq_len // NA_VT_CHUNK
    return pl.pallas_call(
        functools.partial(_na_kernel, n_rows=seq_len // GRID_W),
        grid=(n_pairs, batch),
        in_specs=[
            pl.BlockSpec((None, seq_len, LANES), lambda g, b: (b, 0, g)),
            pl.BlockSpec((None, seq_len, LANES), lambda g, b: (b, 0, n_pairs + g)),
            pl.BlockSpec((chunks, LANES, NA_VT_CHUNK), lambda g, b: (b, g, 0)),
            pl.BlockSpec((2, 3, NA_KEYS, NA_Q), lambda g, b: (g, 0, 0, 0)),
        ],
        out_specs=pl.BlockSpec((None, seq_len, LANES), lambda g, b: (b, 0, g)),
        out_shape=jax.ShapeDtypeStruct((batch, seq_len, NA_HEADS * HEAD_DIM), _BF16),
        scratch_shapes=[pltpu.VMEM((2, 2, NA_KEYS, NA_Q), _F32)],
        compiler_params=_params("parallel", "parallel"),
        name="na_attention",
    )(h3, h3, na_vt, bias)


def _na_bias_table(rpb):
    cols = np.arange(GRID_W)
    col_start = np.clip(cols - NA_WIN_W // 2, 0, GRID_W - NA_WIN_W)
    kc = np.arange(GRID_W)
    in_cols = (kc[:, None] >= col_start[None, :]) & (kc[:, None] < col_start[None, :] + NA_WIN_W)
    col_idx = kc[:, None] - cols[None, :] + (NA_WIN_W - 1)
    i = np.arange(NA_BLOCK_ROWS)
    j = np.arange(NA_SLAB_ROWS)
    slab_off = np.array([0, -NA_WIN_H // 2, -NA_WIN_H])
    win_off = np.stack([np.zeros_like(i), i - NA_WIN_H // 2, np.full_like(i, -NA_WIN_H // 2)])
    key_row = slab_off[:, None] + j[None, :]
    in_rows = ((key_row[:, :, None] >= win_off[:, None, :])
               & (key_row[:, :, None] < win_off[:, None, :] + NA_WIN_H))
    row_idx = key_row[:, :, None] - i[None, None, :] + (NA_WIN_H - 1)
    row_pick = (in_rows[..., None] & (row_idx[..., None] == np.arange(2 * NA_WIN_H - 1))).astype(np.float32)
    col_pick = (in_cols[..., None] & (col_idx[..., None] == np.arange(2 * NA_WIN_W - 1))).astype(np.float32)
    by_col = jnp.einsum("hab,kcb->hakc", rpb, col_pick, precision=lax.Precision.HIGHEST)
    table = jnp.einsum("ljia,hakc->hljkic", row_pick, by_col, precision=lax.Precision.HIGHEST)
    valid = in_rows[:, :, None, :, None] & in_cols[None, None, :, None, :]
    table = jnp.where(valid[None], table * LOG2E, MASK_VALUE)
    return table.reshape(rpb.shape[0], 3, NA_KEYS, NA_Q).astype(_F32)


def _diff_kernel(q_ref, k_ref, vt_ref, lq1_ref, lk1_ref, lq2_ref, lk2_ref, g_ref, o_ref, s_ref, acc_ref,
                 *, seq_len, lambda_init):
    tq, tk = ATT_TQ, ATT_TK
    n_q, n_k = seq_len // tq, seq_len // tk
    d = 2 * HEAD_DIM
    lam = (jnp.exp(jnp.sum(lq1_ref[...] * lk1_ref[...], axis=1, keepdims=True))
           - jnp.exp(jnp.sum(lq2_ref[...] * lk2_ref[...], axis=1, keepdims=True)) + lambda_init)
    lo64 = lax.broadcasted_iota(jnp.int32, (tk, LANES), 1) < HEAD_DIM
    gain = g_ref[...] * (1.0 - lambda_init)

    def scores(qs, c, i):
        kblk = k_ref[tk * i:tk * (i + 1), :]
        zero = jnp.zeros_like(kblk)
        kk = jnp.where(lo64, kblk, zero) if c == 0 else jnp.where(lo64, zero, kblk)
        return lax.dot_general(kk, q_ref[pl.ds(qs, tq), :], _NT, preferred_element_type=_F32)

    def block(qs, qs_next, chunk_max):
        chunk_max = _attend_block(n_k, 2, lambda c, i: scores(qs, c, i), lambda c: scores(qs_next, c, 0),
                                  lambda c, i: vt_ref[i], s_ref, acc_ref, chunk_max)
        o1 = acc_ref[0, 0:d, :] * (1.0 / acc_ref[0, d:d + 1, :])
        o2 = acc_ref[1, 0:d, :] * (1.0 / acc_ref[1, d:d + 1, :])
        o = (o1 - lam * o2).T
        ms = jnp.mean(o * o, axis=1, keepdims=True)
        o_ref[pl.ds(qs, tq), :] = (o * lax.rsqrt(ms + SUBLN_EPS) * gain).astype(_BF16)
        return chunk_max

    first = tuple(_issue_scores(s_ref, 0, c, scores(0, c, 0)) for c in range(2))
    _for_each_query_block(n_q, tq, block, first)


def _diff_attention(h1, vt1, lq1, lk1, lq2, lk2, g_subln, lambda_init, batch, seq_len):
    h3 = h1.reshape(batch, seq_len, H1_W)
    heads = DIFF_HEADS
    small = lambda n: pl.BlockSpec((1, n), lambda b, h: (0, 0))
    return pl.pallas_call(
        functools.partial(_diff_kernel, seq_len=seq_len, lambda_init=lambda_init),
        grid=(batch, heads),
        in_specs=[
            pl.BlockSpec((None, seq_len, LANES), lambda b, h: (b, 0, h)),
            pl.BlockSpec((None, seq_len, LANES), lambda b, h: (b, 0, heads + h)),
            pl.BlockSpec((seq_len // ATT_TK, LANES, ATT_TK), lambda b, h: (b, h, 0)),
            small(HEAD_DIM), small(HEAD_DIM), small(HEAD_DIM), small(HEAD_DIM), small(2 * HEAD_DIM),
        ],
        out_specs=pl.BlockSpec((None, seq_len, LANES), lambda b, h: (b, 0, h)),
        out_shape=jax.ShapeDtypeStruct((batch, seq_len, D_MODEL), _BF16),
        scratch_shapes=[pltpu.VMEM((2, 2, ATT_TK, ATT_TQ), _F32),
                        pltpu.VMEM((2, 2 * HEAD_DIM + ONES_ROWS, ATT_TQ), _F32)],
        compiler_params=_params("parallel", "parallel"),
        name="diff_attention",
    )(h3, h3, vt1, lq1, lk1, lq2, lk2, g_subln)


def _layer_tail_kernel(*refs, n_parts):
    part_refs = refs[:n_parts]
    (x_ref, wo_ref, g1_ref, b1_ref, wg_ref, wu_ref, wd_ref, g2_ref, b2_ref, o_ref) = refs[n_parts:]
    for r in range(x_ref.shape[0] // ROW_TILE):
        rows = slice(ROW_TILE * r, ROW_TILE * (r + 1))
        if n_parts == 1:
            a = part_refs[0][rows, :]
        else:
            a = jnp.concatenate([p[rows, :] for p in part_refs], axis=1)
        mix = jnp.dot(a, wo_ref[...], preferred_element_type=_F32)
        x1 = _layer_norm(ALPHA * x_ref[rows, :] + mix, g1_ref[...], b1_ref[...])
        xb = x1.astype(_BF16)
        hidden = []
        for c in range(N_FF_CHUNKS):
            cols = slice(FF_CHUNK * c, FF_CHUNK * (c + 1))
            gate = jnp.dot(xb, wg_ref[:, cols], preferred_element_type=_F32)
            up = jnp.dot(xb, wu_ref[:, cols], preferred_element_type=_F32)
            hidden.append((gate * jax.nn.sigmoid(gate) * up).astype(_BF16))
        ffn = jnp.dot(jnp.concatenate(hidden, axis=1), wd_ref[...], preferred_element_type=_F32)
        o_ref[rows, :] = _layer_norm(ALPHA * x1 + ffn, g2_ref[...], b2_ref[...])


def _layer_tail(parts, x2d, w_out, g1, b1, wg, wu, wd, g2, b2):
    rows = x2d.shape[0]
    tm = TAIL_TILE
    part_specs = [pl.BlockSpec((tm, p.shape[1]), lambda i: (i, 0)) for p in parts]
    consts = (w_out, g1, b1, wg, wu, wd, g2, b2)
    return pl.pallas_call(
        functools.partial(_layer_tail_kernel, n_parts=len(parts)),
        grid=(rows // tm,),
        in_specs=part_specs + [pl.BlockSpec((tm, D_MODEL), lambda i: (i, 0))] + [_const_spec(c.shape) for c in consts],
        out_specs=pl.BlockSpec((tm, D_MODEL), lambda i: (i, 0)),
        out_shape=jax.ShapeDtypeStruct((rows, D_MODEL), _F32),
        compiler_params=_params("parallel"),
        name="layer_tail",
    )(*parts, x2d, *consts)


def _rope_tables(angles):
    cos = np.cos(angles)
    sin = np.sin(angles)
    return np.concatenate([cos, cos], axis=1), np.concatenate([-sin, sin], axis=1)


def _inv_freq(dim):
    return ROPE_THETA ** (-np.arange(0, dim, 2, dtype=np.float64) / dim)


def _axial_tables(seq_len):
    t = np.arange(seq_len)
    inv = _inv_freq(HEAD_DIM // 2)
    row_c, row_s = _rope_tables((t // GRID_W)[:, None] * inv[None, :])
    col_c, col_s = _rope_tables((t % GRID_W)[:, None] * inv[None, :])
    cos = np.concatenate([row_c, col_c] * (LANES // HEAD_DIM), axis=1)
    sin = np.concatenate([row_s, col_s] * (LANES // HEAD_DIM), axis=1)
    return cos.astype(np.float32), sin.astype(np.float32)


def _seq_tables(seq_len):
    t = np.arange(seq_len)
    cos, sin = _rope_tables(t[:, None] * _inv_freq(HEAD_DIM)[None, :])
    reps = LANES // HEAD_DIM
    return (np.concatenate([cos] * reps, axis=1).astype(np.float32),
            np.concatenate([sin] * reps, axis=1).astype(np.float32))


def _ffn_weights(wg, wu, wd):
    return wg.astype(_BF16), wu.astype(_BF16), wd.astype(_BF16)


def _trunk(x, p):
    batch, seq_len, _ = x.shape
    x2d = x.reshape(batch * seq_len, D_MODEL)
    row = lambda v: v.reshape(1, -1)

    h0, na_vt, vt0 = _proj0(x2d, p["w_in0_a"], p["w_in0_vt"], p["w_in0_b"], p["ax_cos"][:seq_len],
                            p["ax_sin"][:seq_len], p["gqa_gain"], p["ones_bd"], seq_len)
    a_out = _na_attention(h0, na_vt, p["na_bias"], batch, seq_len).reshape(batch * seq_len, -1)
    b_out = _gqa_attention(h0, vt0, batch, seq_len).reshape(batch * seq_len, -1)
    x2d = _layer_tail([a_out, b_out], x2d, p["w_out0"], row(p["ln_mix_g"][0]), row(p["ln_mix_b"][0]),
                      *p["ffn0"], row(p["ln_ffn_g"][0]), row(p["ln_ffn_b"][0]))

    h1, vt1 = _proj1(x2d, p["w_in1_qk"], p["w_in1_vt"], p["seq_cos"][:seq_len], p["seq_sin"][:seq_len], seq_len)
    lambda_init = 0.8 - 0.6 * math.exp(-0.3 * 1)
    c_out = _diff_attention(h1, vt1, p["lq1"], p["lk1"], p["lq2"], p["lk2"], p["g_subln"],
                            lambda_init, batch, seq_len).reshape(batch * seq_len, -1)
    x2d = _layer_tail([c_out], x2d, p["w_out1"], row(p["ln_mix_g"][1]), row(p["ln_mix_b"][1]),
                      *p["ffn1"], row(p["ln_ffn_g"][1]), row(p["ln_ffn_b"][1]))
    return x2d.reshape(batch, seq_len, D_MODEL)


def kernel(x_prompt, x_sample, w_in_mix0, rpb_na, g_q_gqa, g_k_gqa, w_out_mix0, w_in_mix1, lam_q1, lam_k1,
           lam_q2, lam_k2, g_subln, w_out_mix1, ln_mix_g, ln_mix_b, w_ffn_gate, w_ffn_up, w_ffn_down,
           ln_ffn_g, ln_ffn_b):
    max_len = max(x_prompt.shape[1], x_sample.shape[1])
    ax_cos, ax_sin = _axial_tables(max_len)
    seq_cos, seq_sin = _seq_tables(max_len)
    head_of_lane = jnp.arange(LANES) // HEAD_DIM
    w0 = w_in_mix0[0].astype(_BF16)
    na_w = NA_HEADS * HEAD_DIM
    p = {
        "w_in0_a": w0[:, :2 * na_w],
        "w_in0_vt": w0[:, 2 * na_w:3 * na_w].T,
        "w_in0_b": w0[:, 3 * na_w:],
        "ax_cos": ax_cos, "ax_sin": ax_sin, "seq_cos": seq_cos, "seq_sin": seq_sin,
        "gqa_gain": jnp.concatenate([jnp.tile(g_q_gqa[0], GQA_HEADS), jnp.tile(g_k_gqa[0], GQA_KV_HEADS)]
                                    ).reshape(1, -1).astype(_F32),
        "ones_bd": (head_of_lane[:, None] == head_of_lane[None, :]).astype(_BF16),
        "na_bias": _na_bias_table(rpb_na[0]),
        "w_out0": w_out_mix0[0].astype(_BF16),
        "ffn0": _ffn_weights(w_ffn_gate[0], w_ffn_up[0], w_ffn_down[0]),
        "w_in1_qk": w_in_mix1[0][:, :2 * D_MODEL].astype(_BF16),
        "w_in1_vt": w_in_mix1[0][:, 2 * D_MODEL:].T.astype(_BF16),
        "lq1": lam_q1[0].reshape(1, -1), "lk1": lam_k1[0].reshape(1, -1),
        "lq2": lam_q2[0].reshape(1, -1), "lk2": lam_k2[0].reshape(1, -1),
        "g_subln": g_subln[0].reshape(1, -1),
        "w_out1": w_out_mix1[0].astype(_BF16),
        "ffn1": _ffn_weights(w_ffn_gate[1], w_ffn_up[1], w_ffn_down[1]),
        "ln_mix_g": ln_mix_g, "ln_mix_b": ln_mix_b, "ln_ffn_g": ln_ffn_g, "ln_ffn_b": ln_ffn_b,
    }
    return _trunk(x_prompt, p), _trunk(x_sample, p)
```

```python
import functools
import math

import jax
import jax.numpy as jnp
import numpy as np
from jax import lax
from jax.experimental import pallas as pl
from jax.experimental.pallas import tpu as pltpu

D_MODEL = 1024
DEPTH = 2
GRID_W = 64
HEAD_DIM = 64
NA_HEADS = 8
GQA_HEADS = 8
GQA_KV_HEADS = 2
DIFF_HEADS = 8
NA_WIN_H = 8
NA_WIN_W = 16
ROPE_THETA = 10000.0
D_FF = 2816
LN_EPS = 1e-5
RMS_EPS = 1e-6
SUBLN_EPS = 1e-5
ALPHA = (2.0 * DEPTH) ** 0.25
Q_SCALE = HEAD_DIM ** -0.5
LOG2E = math.log2(math.e)

LANES = 128
FF_CHUNK = 256
N_FF_CHUNKS = D_FF // FF_CHUNK
ROW_TILE = 512
TAIL_TILE = 1024
ATT_TQ = 512
ATT_TK = 512
ONES_ROWS = 16
assert ATT_TK == ROW_TILE
MASK_VALUE = -1e30
VMEM_LIMIT = 56 * 1024 * 1024

_BF16 = jnp.bfloat16
_F32 = jnp.float32
_NT = (((1,), (1,)), ((), ()))


def _params(*semantics):
    return pltpu.CompilerParams(dimension_semantics=semantics, vmem_limit_bytes=VMEM_LIMIT)


def _const_spec(shape):
    zeros = (0,) * len(shape)
    return pl.BlockSpec(shape, lambda *_: zeros, pipeline_mode=pl.Buffered(1))


def _layer_norm(xf, gain, bias):
    mu = jnp.mean(xf, axis=-1, keepdims=True)
    xc = xf - mu
    var = jnp.mean(xc * xc, axis=-1, keepdims=True)
    return xc * lax.rsqrt(var + LN_EPS) * gain + bias


def _rotate_pairs(x, half):
    lane = lax.broadcasted_iota(jnp.int32, x.shape, 1)
    first = (lane % (2 * half)) < half
    return jnp.where(first, pltpu.roll(x, LANES - half, 1), pltpu.roll(x, half, 1))


H0_W = 2048
NA_VT_CHUNK = 256


def _proj0_kernel(x_ref, wa_ref, wvt_ref, wb_ref, cos_ref, sin_ref, gain_ref, ones_ref, o_ref, na_vt_ref, vt_ref):
    xb = x_ref[...].astype(_BF16)
    hb = jnp.dot(xb, wb_ref[...], preferred_element_type=_F32)
    cos = cos_ref[...]
    sin = sin_ref[...]
    ones_bd = ones_ref[...]
    lo64 = lax.broadcasted_iota(jnp.int32, cos.shape, 1) < HEAD_DIM

    def norm_rope(xg, gain):
        ssq = jnp.dot((xg * xg).astype(_BF16), ones_bd, preferred_element_type=_F32)
        n = xg * lax.rsqrt(ssq * (1.0 / HEAD_DIM) + RMS_EPS) * gain
        return n * cos + _rotate_pairs(n, 16) * sin

    for g in range(4):
        sl = slice(LANES * g, LANES * (g + 1))
        qg = norm_rope(hb[:, sl], gain_ref[:, sl]) * (Q_SCALE * LOG2E)
        o_ref[:, 1024 + LANES * g:1024 + LANES * (g + 1)] = qg.astype(_BF16)

    k = norm_rope(hb[:, 512:640], gain_ref[:, 512:640])
    kr = pltpu.roll(k, HEAD_DIM, 1)
    zero = jnp.zeros_like(k)
    parts = (jnp.where(lo64, k, zero), jnp.where(lo64, zero, kr),
             jnp.where(lo64, kr, zero), jnp.where(lo64, zero, k))
    for i, part in enumerate(parts):
        o_ref[:, 1536 + LANES * i:1536 + LANES * (i + 1)] = part.astype(_BF16)
    vt_ref[...] = hb[:, 640:768].T.astype(_BF16)

    ha = jnp.dot(xb, wa_ref[...], preferred_element_type=_F32)
    o_ref[:, 0:512] = (ha[:, 0:512] * (Q_SCALE * LOG2E)).astype(_BF16)
    o_ref[:, 512:1024] = ha[:, 512:1024].astype(_BF16)
    na_vt = lax.dot_general(wvt_ref[...], xb, _NT, preferred_element_type=_F32).astype(_BF16)
    for t in range(na_vt_ref.shape[0]):
        na_vt_ref[t] = na_vt[:, NA_VT_CHUNK * t:NA_VT_CHUNK * (t + 1)]


def _proj0(x2d, wa, wvt, wb, cos, sin, gain, ones_bd, seq_len):
    rows = x2d.shape[0]
    tm = ROW_TILE
    pos_blocks = seq_len // tm
    na_chunks = tm // NA_VT_CHUNK
    na_w = NA_HEADS * HEAD_DIM
    return pl.pallas_call(
        _proj0_kernel,
        grid=(rows // tm,),
        in_specs=[
            pl.BlockSpec((tm, D_MODEL), lambda i: (i, 0)),
            _const_spec(wa.shape),
            _const_spec(wvt.shape),
            _const_spec(wb.shape),
            pl.BlockSpec((tm, LANES), lambda i: (i % pos_blocks, 0)),
            pl.BlockSpec((tm, LANES), lambda i: (i % pos_blocks, 0)),
            _const_spec(gain.shape),
            _const_spec(ones_bd.shape),
        ],
        out_specs=[pl.BlockSpec((tm, H0_W), lambda i: (i, 0)),
                   pl.BlockSpec((na_chunks, na_w, NA_VT_CHUNK), lambda i: (i, 0, 0)),
                   pl.BlockSpec((None, LANES, tm), lambda i: (i, 0, 0))],
        out_shape=[jax.ShapeDtypeStruct((rows, H0_W), _BF16),
                   jax.ShapeDtypeStruct((rows // NA_VT_CHUNK, na_w, NA_VT_CHUNK), _BF16),
                   jax.ShapeDtypeStruct((rows // tm, LANES, tm), _BF16)],
        compiler_params=_params("parallel"),
        name="proj0",
    )(x2d, wa, wvt, wb, cos, sin, gain, ones_bd)


H1_W = 2 * D_MODEL


def _proj1_kernel(x_ref, w_ref, wvt_ref, cos_ref, sin_ref, o_ref, vt_ref):
    xb = x_ref[...].astype(_BF16)
    cos = cos_ref[...]
    sin = sin_ref[...]
    for part, scale in ((0, Q_SCALE * LOG2E), (1, 1.0)):
        base = part * D_MODEL
        h = jnp.dot(xb, w_ref[:, base:base + D_MODEL], preferred_element_type=_F32)
        for g in range(D_MODEL // LANES):
            xg = h[:, LANES * g:LANES * (g + 1)]
            r = xg * cos + _rotate_pairs(xg, 32) * sin
            if scale != 1.0:
                r = r * scale
            o_ref[:, base + LANES * g:base + LANES * (g + 1)] = r.astype(_BF16)
    vt_ref[...] = lax.dot_general(wvt_ref[...], xb, _NT, preferred_element_type=_F32).astype(_BF16)


def _proj1(x2d, w_qk, w_vt, cos, sin, seq_len):
    rows = x2d.shape[0]
    tm = ROW_TILE
    pos_blocks = seq_len // tm
    return pl.pallas_call(
        _proj1_kernel,
        grid=(rows // tm,),
        in_specs=[
            pl.BlockSpec((tm, D_MODEL), lambda i: (i, 0)),
            _const_spec(w_qk.shape),
            _const_spec(w_vt.shape),
            pl.BlockSpec((tm, LANES), lambda i: (i % pos_blocks, 0)),
            pl.BlockSpec((tm, LANES), lambda i: (i % pos_blocks, 0)),
        ],
        out_specs=[pl.BlockSpec((tm, H1_W), lambda i: (i, 0)),
                   pl.BlockSpec((None, D_MODEL, tm), lambda i: (i, 0, 0))],
        out_shape=[jax.ShapeDtypeStruct((rows, H1_W), _BF16),
                   jax.ShapeDtypeStruct((rows // tm, D_MODEL, tm), _BF16)],
        compiler_params=_params("parallel"),
        name="proj1",
    )(x2d, w_qk, w_vt, cos, sin)


def _with_ones(vt):
    return jnp.concatenate([vt, jnp.ones((ONES_ROWS, vt.shape[1]), vt.dtype)], axis=0)


def _issue_scores(s_ref, slot, c, s):
    s_ref[slot, c] = s
    return jnp.max(s, axis=0, keepdims=True)


def _attend_block(n_k, n_chain, scores_fn, next_scores_fn, vt_fn, s_ref, acc_ref, chunk_max):
    chunk_max = list(chunk_max)
    m_state = [None] * n_chain
    for i in range(n_k):
        slot = i % 2
        for c in range(n_chain):
            s_next = scores_fn(c, i + 1) if i + 1 < n_k else next_scores_fn(c)
            next_max = _issue_scores(s_ref, 1 - slot, c, s_next)
            m_next = chunk_max[c] if i == 0 else jnp.maximum(m_state[c], chunk_max[c])
            p = jnp.exp2(s_ref[slot, c] - m_next).astype(_BF16)
            pv = jnp.dot(_with_ones(vt_fn(c, i)), p, preferred_element_type=_F32)
            acc_ref[c] = pv if i == 0 else jnp.exp2(m_state[c] - m_next) * acc_ref[c] + pv
            m_state[c] = m_next
            chunk_max[c] = next_max
    return tuple(chunk_max)


def _for_each_query_block(n_q, tq, block, first_max):
    def body(qi, chunk_max):
        qs = pl.multiple_of(qi * tq, tq)
        qs_next = pl.multiple_of(jnp.minimum(qi + 1, n_q - 1) * tq, tq)
        return block(qs, qs_next, chunk_max)

    lax.fori_loop(0, n_q, body, first_max, unroll=2)


def _gqa_kernel(q_ref, k_ref, vt_ref, o_ref, s_ref, acc_ref, *, seq_len):
    tq, tk = ATT_TQ, ATT_TK
    n_q, n_k = seq_len // tq, seq_len // tk
    n_pairs = GQA_HEADS // 2
    d = HEAD_DIM

    def scores(qs, pair, e, i):
        kvh = pair // (n_pairs // GQA_KV_HEADS)
        col = 2 * LANES * kvh + LANES * e
        q = q_ref[pl.ds(qs, tq), LANES * pair:LANES * (pair + 1)]
        return lax.dot_general(k_ref[tk * i:tk * (i + 1), col:col + LANES], q, _NT, preferred_element_type=_F32)

    def q_body(qi, chunk_max):
        qs = pl.multiple_of(qi * tq, tq)
        qs_next = pl.multiple_of(jnp.minimum(qi + 1, n_q - 1) * tq, tq)
        for pair in range(n_pairs):
            kvh = pair // (n_pairs // GQA_KV_HEADS)
            nxt = (qs, pair + 1) if pair + 1 < n_pairs else (qs_next, 0)
            chunk_max = _attend_block(
                n_k, 2,
                lambda e, i, pair=pair: scores(qs, pair, e, i),
                lambda e, nxt=nxt: scores(nxt[0], nxt[1], e, 0),
                lambda e, i: vt_ref[i],
                s_ref, acc_ref, chunk_max)
            halves = [acc_ref[e, d * kvh:d * (kvh + 1), :] * (1.0 / acc_ref[e, 2 * d:2 * d + 1, :])
                      for e in range(2)]
            out = jnp.concatenate(halves, axis=0).T
            o_ref[pl.ds(qs, tq), LANES * pair:LANES * (pair + 1)] = out.astype(_BF16)
        return chunk_max

    first = tuple(_issue_scores(s_ref, 0, e, scores(0, 0, e, 0)) for e in range(2))
    lax.fori_loop(0, n_q, q_body, first)


def _gqa_attention(h0, vt0, batch, seq_len):
    h3 = h0.reshape(batch, seq_len, H0_W)
    width = GQA_HEADS * HEAD_DIM
    return pl.pallas_call(
        functools.partial(_gqa_kernel, seq_len=seq_len),
        grid=(batch,),
        in_specs=[
            pl.BlockSpec((None, seq_len, width), lambda b: (b, 0, 2)),
            pl.BlockSpec((None, seq_len, width), lambda b: (b, 0, 3)),
            pl.BlockSpec((seq_len // ATT_TK, LANES, ATT_TK), lambda b: (b, 0, 0)),
        ],
        out_specs=pl.BlockSpec((None, seq_len, width), lambda b: (b, 0, 0)),
        out_shape=jax.ShapeDtypeStruct((batch, seq_len, width), _BF16),
        scratch_shapes=[pltpu.VMEM((2, 2, ATT_TK, ATT_TQ), _F32),
                        pltpu.VMEM((2, GQA_KV_HEADS * HEAD_DIM + ONES_ROWS, ATT_TQ), _F32)],
        compiler_params=_params("parallel"),
        name="gqa_attention",
    )(h3, h3, vt0)


NA_BLOCK_ROWS = 4
NA_SLAB_ROWS = NA_WIN_H + NA_BLOCK_ROWS
NA_Q = NA_BLOCK_ROWS * GRID_W
NA_KEYS = NA_SLAB_ROWS * GRID_W
assert NA_Q == NA_VT_CHUNK and NA_KEYS % NA_VT_CHUNK == 0


def _na_kernel(q_ref, k_ref, vt_ref, bias_ref, o_ref, s_ref, *, n_rows):
    n_blocks = n_rows // NA_BLOCK_ROWS
    lo64 = lax.broadcasted_iota(jnp.int32, (NA_KEYS, LANES), 1) < HEAD_DIM
    d = HEAD_DIM

    def slab_start(bi):
        return jnp.clip(bi - 1, 0, n_blocks - NA_SLAB_ROWS // NA_BLOCK_ROWS)

    def biased_scores(bi, e):
        q = q_ref[pl.ds(pl.multiple_of(bi * NA_Q, NA_Q), NA_Q), :]
        kslab = k_ref[pl.ds(pl.multiple_of(slab_start(bi) * NA_Q, NA_Q), NA_KEYS), :]
        zero = jnp.zeros_like(kslab)
        kk = jnp.where(lo64, kslab, zero) if e == 0 else jnp.where(lo64, zero, kslab)
        layout = jnp.where(bi == 0, 0, jnp.where(bi == n_blocks - 1, 2, 1))
        return lax.dot_general(kk, q, _NT, preferred_element_type=_F32) + bias_ref[e, layout]

    def attend(bi, e, slot, s_max):
        p = jnp.exp2(s_ref[slot, e] - s_max).astype(_BF16)
        vt = jnp.concatenate([vt_ref[slab_start(bi) + t, d * e:d * (e + 1), :]
                              for t in range(NA_KEYS // NA_VT_CHUNK)], axis=1)
        acc = jnp.dot(_with_ones(vt), p, preferred_element_type=_F32)
        return acc[0:d, :] * (1.0 / acc[d:d + 1, :])

    def two_blocks(j, s_max):
        b0 = 2 * j
        blocks = [b0, b0 + 1, jnp.minimum(b0 + 2, n_blocks - 1)]
        s_max = list(s_max)
        for u in range(2):
            halves = []
            for e in range(2):
                next_max = _issue_scores(s_ref, 1 - u, e, biased_scores(blocks[u + 1], e))
                halves.append(attend(blocks[u], e, u, s_max[e]))
                s_max[e] = next_max
            qs = pl.multiple_of(blocks[u] * NA_Q, NA_Q)
            o_ref[pl.ds(qs, NA_Q), :] = jnp.concatenate(halves, axis=0).T.astype(_BF16)
        return tuple(s_max)

    first = tuple(_issue_scores(s_ref, 0, e, biased_scores(0, e)) for e in range(2))
    lax.fori_loop(0, n_blocks // 2, two_blocks, first)


def _na_attention(h0, na_vt, bias, batch, seq_len):
    h3 = h0.reshape(batch, seq_len, H0_W)
    n_pairs = NA_HEADS // 2
    chunks = seq_len // NA_VT_CHUNK
    return pl.pallas_call(
        functools.partial(_na_kernel, n_rows=seq_len // GRID_W),
        grid=(n_pairs, batch),
        in_specs=[
            pl.BlockSpec((None, seq_len, LANES), lambda g, b: (b, 0, g)),
            pl.BlockSpec((None, seq_len, LANES), lambda g, b: (b, 0, n_pairs + g)),
            pl.BlockSpec((chunks, LANES, NA_VT_CHUNK), lambda g, b: (b, g, 0)),
            pl.BlockSpec((2, 3, NA_KEYS, NA_Q), lambda g, b: (g, 0, 0, 0)),
        ],
        out_specs=pl.BlockSpec((None, seq_len, LANES), lambda g, b: (b, 0, g)),
        out_shape=jax.ShapeDtypeStruct((batch, seq_len, NA_HEADS * HEAD_DIM), _BF16),
        scratch_shapes=[pltpu.VMEM((2, 2, NA_KEYS, NA_Q), _F32)],
        compiler_params=_params("parallel", "parallel"),
        name="na_attention",
    )(h3, h3, na_vt, bias)


def _na_bias_table(rpb):
    cols = np.arange(GRID_W)
    col_start = np.clip(cols - NA_WIN_W // 2, 0, GRID_W - NA_WIN_W)
    kc = np.arange(GRID_W)
    in_cols = (kc[:, None] >= col_start[None, :]) & (kc[:, None] < col_start[None, :] + NA_WIN_W)
    col_idx = kc[:, None] - cols[None, :] + (NA_WIN_W - 1)
    i = np.arange(NA_BLOCK_ROWS)
    j = np.arange(NA_SLAB_ROWS)
    slab_off = np.array([0, -NA_WIN_H // 2, -NA_WIN_H])
    win_off = np.stack([np.zeros_like(i), i - NA_WIN_H // 2, np.full_like(i, -NA_WIN_H // 2)])
    key_row = slab_off[:, None] + j[None, :]
    in_rows = ((key_row[:, :, None] >= win_off[:, None, :])
               & (key_row[:, :, None] < win_off[:, None, :] + NA_WIN_H))
    row_idx = key_row[:, :, None] - i[None, None, :] + (NA_WIN_H - 1)
    row_pick = (in_rows[..., None] & (row_idx[..., None] == np.arange(2 * NA_WIN_H - 1))).astype(np.float32)
    col_pick = (in_cols[..., None] & (col_idx[..., None] == np.arange(2 * NA_WIN_W - 1))).astype(np.float32)
    by_col = jnp.einsum("hab,kcb->hakc", rpb, col_pick, precision=lax.Precision.HIGHEST)
    table = jnp.einsum("ljia,hakc->hljkic", row_pick, by_col, precision=lax.Precision.HIGHEST)
    valid = in_rows[:, :, None, :, None] & in_cols[None, None, :, None, :]
    table = jnp.where(valid[None], table * LOG2E, MASK_VALUE)
    return table.reshape(rpb.shape[0], 3, NA_KEYS, NA_Q).astype(_F32)


def _diff_kernel(q_ref, k_ref, vt_ref, lq1_ref, lk1_ref, lq2_ref, lk2_ref, g_ref, o_ref, s_ref, acc_ref,
                 *, seq_len, lambda_init):
    tq, tk = ATT_TQ, ATT_TK
    n_q, n_k = seq_len // tq, seq_len // tk
    d = 2 * HEAD_DIM
    lam = (jnp.exp(jnp.sum(lq1_ref[...] * lk1_ref[...], axis=1, keepdims=True))
           - jnp.exp(jnp.sum(lq2_ref[...] * lk2_ref[...], axis=1, keepdims=True)) + lambda_init)
    lo64 = lax.broadcasted_iota(jnp.int32, (tk, LANES), 1) < HEAD_DIM
    gain = g_ref[...] * (1.0 - lambda_init)

    def scores(qs, c, i):
        kblk = k_ref[tk * i:tk * (i + 1), :]
        zero = jnp.zeros_like(kblk)
        kk = jnp.where(lo64, kblk, zero) if c == 0 else jnp.where(lo64, zero, kblk)
        return lax.dot_general(kk, q_ref[pl.ds(qs, tq), :], _NT, preferred_element_type=_F32)

    def block(qs, qs_next, chunk_max):
        chunk_max = _attend_block(n_k, 2, lambda c, i: scores(qs, c, i), lambda c: scores(qs_next, c, 0),
                                  lambda c, i: vt_ref[i], s_ref, acc_ref, chunk_max)
        o1 = acc_ref[0, 0:d, :] * (1.0 / acc_ref[0, d:d + 1, :])
        o2 = acc_ref[1, 0:d, :] * (1.0 / acc_ref[1, d:d + 1, :])
        o = (o1 - lam * o2).T
        ms = jnp.mean(o * o, axis=1, keepdims=True)
        o_ref[pl.ds(qs, tq), :] = (o * lax.rsqrt(ms + SUBLN_EPS) * gain).astype(_BF16)
        return chunk_max

    first = tuple(_issue_scores(s_ref, 0, c, scores(0, c, 0)) for c in range(2))
    _for_each_query_block(n_q, tq, block, first)


def _diff_attention(h1, vt1, lq1, lk1, lq2, lk2, g_subln, lambda_init, batch, seq_len):
    h3 = h1.reshape(batch, seq_len, H1_W)
    heads = DIFF_HEADS
    small = lambda n: pl.BlockSpec((1, n), lambda b, h: (0, 0))
    return pl.pallas_call(
        functools.partial(_diff_kernel, seq_len=seq_len, lambda_init=lambda_init),
        grid=(batch, heads),
        in_specs=[
            pl.BlockSpec((None, seq_len, LANES), lambda b, h: (b, 0, h)),
            pl.BlockSpec((None, seq_len, LANES), lambda b, h: (b, 0, heads + h)),
            pl.BlockSpec((seq_len // ATT_TK, LANES, ATT_TK), lambda b, h: (b, h, 0)),
            small(HEAD_DIM), small(HEAD_DIM), small(HEAD_DIM), small(HEAD_DIM), small(2 * HEAD_DIM),
        ],
        out_specs=pl.BlockSpec((None, seq_len, LANES), lambda b, h: (b, 0, h)),
        out_shape=jax.ShapeDtypeStruct((batch, seq_len, D_MODEL), _BF16),
        scratch_shapes=[pltpu.VMEM((2, 2, ATT_TK, ATT_TQ), _F32),
                        pltpu.VMEM((2, 2 * HEAD_DIM + ONES_ROWS, ATT_TQ), _F32)],
        compiler_params=_params("parallel", "parallel"),
        name="diff_attention",
    )(h3, h3, vt1, lq1, lk1, lq2, lk2, g_subln)


def _layer_tail_kernel(*refs, n_parts):
    part_refs = refs[:n_parts]
    (x_ref, wo_ref, g1_ref, b1_ref, wg_ref, wu_ref, wd_ref, g2_ref, b2_ref, o_ref) = refs[n_parts:]
    for r in range(x_ref.shape[0] // ROW_TILE):
        rows = slice(ROW_TILE * r, ROW_TILE * (r + 1))
        if n_parts == 1:
            a = part_refs[0][rows, :]
        else:
            a = jnp.concatenate([p[rows, :] for p in part_refs], axis=1)
        mix = jnp.dot(a, wo_ref[...], preferred_element_type=_F32)
        x1 = _layer_norm(ALPHA * x_ref[rows, :] + mix, g1_ref[...], b1_ref[...])
        xb = x1.astype(_BF16)
        hidden = []
        for c in range(N_FF_CHUNKS):
            cols = slice(FF_CHUNK * c, FF_CHUNK * (c + 1))
            gate = jnp.dot(xb, wg_ref[:, cols], preferred_element_type=_F32)
            up = jnp.dot(xb, wu_ref[:, cols], preferred_element_type=_F32)
            hidden.append((gate * jax.nn.sigmoid(gate) * up).astype(_BF16))
        ffn = jnp.dot(jnp.concatenate(hidden, axis=1), wd_ref[...], preferred_element_type=_F32)
        o_ref[rows, :] = _layer_norm(ALPHA * x1 + ffn, g2_ref[...], b2_ref[...])


def _layer_tail(parts, x2d, w_out, g1, b1, wg, wu, wd, g2, b2):
    rows = x2d.shape[0]
    tm = TAIL_TILE
    part_specs = [pl.BlockSpec((tm, p.shape[1]), lambda i: (i, 0)) for p in parts]
    consts = (w_out, g1, b1, wg, wu, wd, g2, b2)
    return pl.pallas_call(
        functools.partial(_layer_tail_kernel, n_parts=len(parts)),
        grid=(rows // tm,),
        in_specs=part_specs + [pl.BlockSpec((tm, D_MODEL), lambda i: (i, 0))] + [_const_spec(c.shape) for c in consts],
        out_specs=pl.BlockSpec((tm, D_MODEL), lambda i: (i, 0)),
        out_shape=jax.ShapeDtypeStruct((rows, D_MODEL), _F32),
        compiler_params=_params("parallel"),
        name="layer_tail",
    )(*parts, x2d, *consts)


def _rope_tables(angles):
    cos = np.cos(angles)
    sin = np.sin(angles)
    return np.concatenate([cos, cos], axis=1), np.concatenate([-sin, sin], axis=1)


def _inv_freq(dim):
    return ROPE_THETA ** (-np.arange(0, dim, 2, dtype=np.float64) / dim)


def _axial_tables(seq_len):
    t = np.arange(seq_len)
    inv = _inv_freq(HEAD_DIM // 2)
    row_c, row_s = _rope_tables((t // GRID_W)[:, None] * inv[None, :])
    col_c, col_s = _rope_tables((t % GRID_W)[:, None] * inv[None, :])
    cos = np.concatenate([row_c, col_c] * (LANES // HEAD_DIM), axis=1)
    sin = np.concatenate([row_s, col_s] * (LANES // HEAD_DIM), axis=1)
    return cos.astype(np.float32), sin.astype(np.float32)


def _seq_tables(seq_len):
    t = np.arange(seq_len)
    cos, sin = _rope_tables(t[:, None] * _inv_freq(HEAD_DIM)[None, :])
    reps = LANES // HEAD_DIM
    return (np.concatenate([cos] * reps, axis=1).astype(np.float32),
            np.concatenate([sin] * reps, axis=1).astype(np.float32))


def _ffn_weights(wg, wu, wd):
    return wg.astype(_BF16), wu.astype(_BF16), wd.astype(_BF16)


def _trunk(x, p):
    batch, seq_len, _ = x.shape
    x2d = x.reshape(batch * seq_len, D_MODEL)
    row = lambda v: v.reshape(1, -1)

    h0, na_vt, vt0 = _proj0(x2d, p["w_in0_a"], p["w_in0_vt"], p["w_in0_b"], p["ax_cos"][:seq_len],
                            p["ax_sin"][:seq_len], p["gqa_gain"], p["ones_bd"], seq_len)
    a_out = _na_attention(h0, na_vt, p["na_bias"], batch, seq_len).reshape(batch * seq_len, -1)
    b_out = _gqa_attention(h0, vt0, batch, seq_len).reshape(batch * seq_len, -1)
    x2d = _layer_tail([a_out, b_out], x2d, p["w_out0"], row(p["ln_mix_g"][0]), row(p["ln_mix_b"][0]),
                      *p["ffn0"], row(p["ln_ffn_g"][0]), row(p["ln_ffn_b"][0]))

    h1, vt1 = _proj1(x2d, p["w_in1_qk"], p["w_in1_vt"], p["seq_cos"][:seq_len], p["seq_sin"][:seq_len], seq_len)
    lambda_init = 0.8 - 0.6 * math.exp(-0.3 * 1)
    c_out = _diff_attention(h1, vt1, p["lq1"], p["lk1"], p["lq2"], p["lk2"], p["g_subln"],
                            lambda_init, batch, seq_len).reshape(batch * seq_len, -1)
    x2d = _layer_tail([c_out], x2d, p["w_out1"], row(p["ln_mix_g"][1]), row(p["ln_mix_b"][1]),
                      *p["ffn1"], row(p["ln_ffn_g"][1]), row(p["ln_ffn_b"][1]))
    return x2d.reshape(batch, seq_len, D_MODEL)


def kernel(x_prompt, x_sample, w_in_mix0, rpb_na, g_q_gqa, g_k_gqa, w_out_mix0, w_in_mix1, lam_q1, lam_k1,
           lam_q2, lam_k2, g_subln, w_out_mix1, ln_mix_g, ln_mix_b, w_ffn_gate, w_ffn_up, w_ffn_down,
           ln_ffn_g, ln_ffn_b):
    max_len = max(x_prompt.shape[1], x_sample.shape[1])
    ax_cos, ax_sin = _axial_tables(max_len)
    seq_cos, seq_sin = _seq_tables(max_len)
    head_of_lane = jnp.arange(LANES) // HEAD_DIM
    w0 = w_in_mix0[0].astype(_BF16)
    na_w = NA_HEADS * HEAD_DIM
    p = {
        "w_in0_a": w0[:, :2 * na_w],
        "w_in0_vt": w0[:, 2 * na_w:3 * na_w].T,
        "w_in0_b": w0[:, 3 * na_w:],
        "ax_cos": ax_cos, "ax_sin": ax_sin, "seq_cos": seq_cos, "seq_sin": seq_sin,
        "gqa_gain": jnp.concatenate([jnp.tile(g_q_gqa[0], GQA_HEADS), jnp.tile(g_k_gqa[0], GQA_KV_HEADS)]
                                    ).reshape(1, -1).astype(_F32),
        "ones_bd": (head_of_lane[:, None] == head_of_lane[None, :]).astype(_BF16),
        "na_bias": _na_bias_table(rpb_na[0]),
        "w_out0": w_out_mix0[0].astype(_BF16),
        "ffn0": _ffn_weights(w_ffn_gate[0], w_ffn_up[0], w_ffn_down[0]),
        "w_in1_qk": w_in_mix1[0][:, :2 * D_MODEL].astype(_BF16),
        "w_in1_vt": w_in_mix1[0][:, 2 * D_MODEL:].T.astype(_BF16),
        "lq1": lam_q1[0].reshape(1, -1), "lk1": lam_k1[0].reshape(1, -1),
        "lq2": lam_q2[0].reshape(1, -1), "lk2": lam_k2[0].reshape(1, -1),
        "g_subln": g_subln[0].reshape(1, -1),
        "w_out1": w_out_mix1[0].astype(_BF16),
        "ffn1": _ffn_weights(w_ffn_gate[1], w_ffn_up[1], w_ffn_down[1]),
        "ln_mix_g": ln_mix_g, "ln_mix_b": ln_mix_b, "ln_ffn_g": ln_ffn_g, "ln_ffn_b": ln_ffn_b,
    }
    return _trunk(x_prompt, p), _trunk(x_sample, p)
```

```python
import functools
import math

import jax
import jax.numpy as jnp
import numpy as np
from jax import lax
from jax.experimental import pallas as pl
from jax.experimental.pallas import tpu as pltpu

D_MODEL = 1024
DEPTH = 2
GRID_W = 64
HEAD_DIM = 64
NA_HEADS = 8
GQA_HEADS = 8
GQA_KV_HEADS = 2
DIFF_HEADS = 8
NA_WIN_H = 8
NA_WIN_W = 16
ROPE_THETA = 10000.0
D_FF = 2816
LN_EPS = 1e-5
RMS_EPS = 1e-6
SUBLN_EPS = 1e-5
ALPHA = (2.0 * DEPTH) ** 0.25
Q_SCALE = HEAD_DIM ** -0.5
LOG2E = math.log2(math.e)

LANES = 128
FF_CHUNK = 256
N_FF_CHUNKS = D_FF // FF_CHUNK
ROW_TILE = 512
TAIL_TILE = 1024
ATT_TQ = 512
ATT_TK = 512
ONES_ROWS = 16
assert ATT_TK == ROW_TILE
MASK_VALUE = -1e30
VMEM_LIMIT = 56 * 1024 * 1024

_BF16 = jnp.bfloat16
_F32 = jnp.float32
_NT = (((1,), (1,)), ((), ()))


def _params(*semantics):
    return pltpu.CompilerParams(dimension_semantics=semantics, vmem_limit_bytes=VMEM_LIMIT)


def _const_spec(shape):
    zeros = (0,) * len(shape)
    return pl.BlockSpec(shape, lambda *_: zeros, pipeline_mode=pl.Buffered(1))


def _layer_norm(xf, gain, bias):
    mu = jnp.mean(xf, axis=-1, keepdims=True)
    xc = xf - mu
    var = jnp.mean(xc * xc, axis=-1, keepdims=True)
    return xc * lax.rsqrt(var + LN_EPS) * gain + bias


def _rotate_pairs(x, half):
    lane = lax.broadcasted_iota(jnp.int32, x.shape, 1)
    first = (lane % (2 * half)) < half
    return jnp.where(first, pltpu.roll(x, LANES - half, 1), pltpu.roll(x, half, 1))


H0_W = 2048
NA_VT_CHUNK = 256


def _proj0_kernel(x_ref, wa_ref, wvt_ref, wb_ref, cos_ref, sin_ref, gain_ref, ones_ref, o_ref, na_vt_ref, vt_ref):
    xb = x_ref[...].astype(_BF16)
    hb = jnp.dot(xb, wb_ref[...], preferred_element_type=_F32)
    cos = cos_ref[...]
    sin = sin_ref[...]
    ones_bd = ones_ref[...]
    lo64 = lax.broadcasted_iota(jnp.int32, cos.shape, 1) < HEAD_DIM

    def norm_rope(xg, gain):
        ssq = jnp.dot((xg * xg).astype(_BF16), ones_bd, preferred_element_type=_F32)
        n = xg * lax.rsqrt(ssq * (1.0 / HEAD_DIM) + RMS_EPS) * gain
        return n * cos + _rotate_pairs(n, 16) * sin

    for g in range(4):
        sl = slice(LANES * g, LANES * (g + 1))
        qg = norm_rope(hb[:, sl], gain_ref[:, sl]) * (Q_SCALE * LOG2E)
        o_ref[:, 1024 + LANES * g:1024 + LANES * (g + 1)] = qg.astype(_BF16)

    k = norm_rope(hb[:, 512:640], gain_ref[:, 512:640])
    kr = pltpu.roll(k, HEAD_DIM, 1)
    zero = jnp.zeros_like(k)
    parts = (jnp.where(lo64, k, zero), jnp.where(lo64, zero, kr),
             jnp.where(lo64, kr, zero), jnp.where(lo64, zero, k))
    for i, part in enumerate(parts):
        o_ref[:, 1536 + LANES * i:1536 + LANES * (i + 1)] = part.astype(_BF16)
    vt_ref[...] = hb[:, 640:768].T.astype(_BF16)

    ha = jnp.dot(xb, wa_ref[...], preferred_element_type=_F32)
    o_ref[:, 0:512] = (ha[:, 0:512] * (Q_SCALE * LOG2E)).astype(_BF16)
    o_ref[:, 512:1024] = ha[:, 512:1024].astype(_BF16)
    na_vt = lax.dot_general(wvt_ref[...], xb, _NT, preferred_element_type=_F32).astype(_BF16)
    for t in range(na_vt_ref.shape[0]):
        na_vt_ref[t] = na_vt[:, NA_VT_CHUNK * t:NA_VT_CHUNK * (t + 1)]


def _proj0(x2d, wa, wvt, wb, cos, sin, gain, ones_bd, seq_len):
    rows = x2d.shape[0]
    tm = ROW_TILE
    pos_blocks = seq_len // tm
    na_chunks = tm // NA_VT_CHUNK
    na_w = NA_HEADS * HEAD_DIM
    return pl.pallas_call(
        _proj0_kernel,
        grid=(rows // tm,),
        in_specs=[
            pl.BlockSpec((tm, D_MODEL), lambda i: (i, 0)),
            _const_spec(wa.shape),
            _const_spec(wvt.shape),
            _const_spec(wb.shape),
            pl.BlockSpec((tm, LANES), lambda i: (i % pos_blocks, 0)),
            pl.BlockSpec((tm, LANES), lambda i: (i % pos_blocks, 0)),
            _const_spec(gain.shape),
            _const_spec(ones_bd.shape),
        ],
        out_specs=[pl.BlockSpec((tm, H0_W), lambda i: (i, 0)),
                   pl.BlockSpec((na_chunks, na_w, NA_VT_CHUNK), lambda i: (i, 0, 0)),
                   pl.BlockSpec((None, LANES, tm), lambda i: (i, 0, 0))],
        out_shape=[jax.ShapeDtypeStruct((rows, H0_W), _BF16),
                   jax.ShapeDtypeStruct((rows // NA_VT_CHUNK, na_w, NA_VT_CHUNK), _BF16),
                   jax.ShapeDtypeStruct((rows // tm, LANES, tm), _BF16)],
        compiler_params=_params("parallel"),
        name="proj0",
    )(x2d, wa, wvt, wb, cos, sin, gain, ones_bd)


H1_W = 2 * D_MODEL


def _proj1_kernel(x_ref, w_ref, wvt_ref, cos_ref, sin_ref, o_ref, vt_ref):
    xb = x_ref[...].astype(_BF16)
    cos = cos_ref[...]
    sin = sin_ref[...]
    for part, scale in ((0, Q_SCALE * LOG2E), (1, 1.0)):
        base = part * D_MODEL
        h = jnp.dot(xb, w_ref[:, base:base + D_MODEL], preferred_element_type=_F32)
        for g in range(D_MODEL // LANES):
            xg = h[:, LANES * g:LANES * (g + 1)]
            r = xg * cos + _rotate_pairs(xg, 32) * sin
            if scale != 1.0:
                r = r * scale
            o_ref[:, base + LANES * g:base + LANES * (g + 1)] = r.astype(_BF16)
    vt_ref[...] = lax.dot_general(wvt_ref[...], xb, _NT, preferred_element_type=_F32).astype(_BF16)


def _proj1(x2d, w_qk, w_vt, cos, sin, seq_len):
    rows = x2d.shape[0]
    tm = ROW_TILE
    pos_blocks = seq_len // tm
    return pl.pallas_call(
        _proj1_kernel,
        grid=(rows // tm,),
        in_specs=[
            pl.BlockSpec((tm, D_MODEL), lambda i: (i, 0)),
            _const_spec(w_qk.shape),
            _const_spec(w_vt.shape),
            pl.BlockSpec((tm, LANES), lambda i: (i % pos_blocks, 0)),
            pl.BlockSpec((tm, LANES), lambda i: (i % pos_blocks, 0)),
        ],
        out_specs=[pl.BlockSpec((tm, H1_W), lambda i: (i, 0)),
                   pl.BlockSpec((None, D_MODEL, tm), lambda i: (i, 0, 0))],
        out_shape=[jax.ShapeDtypeStruct((rows, H1_W), _BF16),
                   jax.ShapeDtypeStruct((rows // tm, D_MODEL, tm), _BF16)],
        compiler_params=_params("parallel"),
        name="proj1",
    )(x2d, w_qk, w_vt, cos, sin)


def _with_ones(vt):
    return jnp.concatenate([vt, jnp.ones((ONES_ROWS, vt.shape[1]), vt.dtype)], axis=0)


def _issue_scores(s_ref, slot, c, s):
    s_ref[slot, c] = s
    return jnp.max(s, axis=0, keepdims=True)


def _attend_block(n_k, n_chain, scores_fn, next_scores_fn, vt_fn, s_ref, acc_ref, chunk_max):
    chunk_max = list(chunk_max)
    m_state = [None] * n_chain
    for i in range(n_k):
        slot = i % 2
        for c in range(n_chain):
            s_next = scores_fn(c, i + 1) if i + 1 < n_k else next_scores_fn(c)
            next_max = _issue_scores(s_ref, 1 - slot, c, s_next)
            m_next = chunk_max[c] if i == 0 else jnp.maximum(m_state[c], chunk_max[c])
            p = jnp.exp2(s_ref[slot, c] - m_next).astype(_BF16)
            pv = jnp.dot(_with_ones(vt_fn(c, i)), p, preferred_element_type=_F32)
            acc_ref[c] = pv if i == 0 else jnp.exp2(m_state[c] - m_next) * acc_ref[c] + pv
            m_state[c] = m_next
            chunk_max[c] = next_max
    return tuple(chunk_max)


def _for_each_query_block(n_q, tq, block, first_max):
    def body(qi, chunk_max):
        qs = pl.multiple_of(qi * tq, tq)
        qs_next = pl.multiple_of(jnp.minimum(qi + 1, n_q - 1) * tq, tq)
        return block(qs, qs_next, chunk_max)

    lax.fori_loop(0, n_q, body, first_max, unroll=2)


def _gqa_kernel(q_ref, k_ref, vt_ref, o_ref, s_ref, acc_ref, *, seq_len):
    tq, tk = ATT_TQ, ATT_TK
    n_q, n_k = seq_len // tq, seq_len // tk
    n_pairs = GQA_HEADS // 2
    d = HEAD_DIM

    def scores(qs, pair, e, i):
        kvh = pair // (n_pairs // GQA_KV_HEADS)
        col = 2 * LANES * kvh + LANES * e
        q = q_ref[pl.ds(qs, tq), LANES * pair:LANES * (pair + 1)]
        return lax.dot_general(k_ref[tk * i:tk * (i + 1), col:col + LANES], q, _NT, preferred_element_type=_F32)

    def q_body(qi, chunk_max):
        qs = pl.multiple_of(qi * tq, tq)
        qs_next = pl.multiple_of(jnp.minimum(qi + 1, n_q - 1) * tq, tq)
        for pair in range(n_pairs):
            kvh = pair // (n_pairs // GQA_KV_HEADS)
            nxt = (qs, pair + 1) if pair + 1 < n_pairs else (qs_next, 0)
            chunk_max = _attend_block(
                n_k, 2,
                lambda e, i, pair=pair: scores(qs, pair, e, i),
                lambda e, nxt=nxt: scores(nxt[0], nxt[1], e, 0),
                lambda e, i: vt_ref[i],
                s_ref, acc_ref, chunk_max)
            halves = [acc_ref[e, d * kvh:d * (kvh + 1), :] * (1.0 / acc_ref[e, 2 * d:2 * d + 1, :])
                      for e in range(2)]
            out = jnp.concatenate(halves, axis=0).T
            o_ref[pl.ds(qs, tq), LANES * pair:LANES * (pair + 1)] = out.astype(_BF16)
        return chunk_max

    first = tuple(_issue_scores(s_ref, 0, e, scores(0, 0, e, 0)) for e in range(2))
    lax.fori_loop(0, n_q, q_body, first)


def _gqa_attention(h0, vt0, batch, seq_len):
    h3 = h0.reshape(batch, seq_len, H0_W)
    width = GQA_HEADS * HEAD_DIM
    return pl.pallas_call(
        functools.partial(_gqa_kernel, seq_len=seq_len),
        grid=(batch,),
        in_specs=[
            pl.BlockSpec((None, seq_len, width), lambda b: (b, 0, 2)),
            pl.BlockSpec((None, seq_len, width), lambda b: (b, 0, 3)),
            pl.BlockSpec((seq_len // ATT_TK, LANES, ATT_TK), lambda b: (b, 0, 0)),
        ],
        out_specs=pl.BlockSpec((None, seq_len, width), lambda b: (b, 0, 0)),
        out_shape=jax.ShapeDtypeStruct((batch, seq_len, width), _BF16),
        scratch_shapes=[pltpu.VMEM((2, 2, ATT_TK, ATT_TQ), _F32),
                        pltpu.VMEM((2, GQA_KV_HEADS * HEAD_DIM + ONES_ROWS, ATT_TQ), _F32)],
        compiler_params=_params("parallel"),
        name="gqa_attention",
    )(h3, h3, vt0)


NA_BLOCK_ROWS = 4
NA_SLAB_ROWS = NA_WIN_H + NA_BLOCK_ROWS
NA_Q = NA_BLOCK_ROWS * GRID_W
NA_KEYS = NA_SLAB_ROWS * GRID_W
assert NA_Q == NA_VT_CHUNK and NA_KEYS % NA_VT_CHUNK == 0


def _na_kernel(q_ref, k_ref, vt_ref, bias_ref, o_ref, s_ref, *, n_rows):
    n_blocks = n_rows // NA_BLOCK_ROWS
    lo64 = lax.broadcasted_iota(jnp.int32, (NA_KEYS, LANES), 1) < HEAD_DIM
    d = HEAD_DIM

    def slab_start(bi):
        return jnp.clip(bi - 1, 0, n_blocks - NA_SLAB_ROWS // NA_BLOCK_ROWS)

    def biased_scores(bi, e):
        q = q_ref[pl.ds(pl.multiple_of(bi * NA_Q, NA_Q), NA_Q), :]
        kslab = k_ref[pl.ds(pl.multiple_of(slab_start(bi) * NA_Q, NA_Q), NA_KEYS), :]
        zero = jnp.zeros_like(kslab)
        kk = jnp.where(lo64, kslab, zero) if e == 0 else jnp.where(lo64, zero, kslab)
        layout = jnp.where(bi == 0, 0, jnp.where(bi == n_blocks - 1, 2, 1))
        return lax.dot_general(kk, q, _NT, preferred_element_type=_F32) + bias_ref[e, layout]

    def attend(bi, e, slot, s_max):
        p = jnp.exp2(s_ref[slot, e] - s_max).astype(_BF16)
        vt = jnp.concatenate([vt_ref[slab_start(bi) + t, d * e:d * (e + 1), :]
                              for t in range(NA_KEYS // NA_VT_CHUNK)], axis=1)
        acc = jnp.dot(_with_ones(vt), p, preferred_element_type=_F32)
        return acc[0:d, :] * (1.0 / acc[d:d + 1, :])

    def two_blocks(j, s_max):
        b0 = 2 * j
        blocks = [b0, b0 + 1, jnp.minimum(b0 + 2, n_blocks - 1)]
        s_max = list(s_max)
        for u in range(2):
            halves = []
            for e in range(2):
                next_max = _issue_scores(s_ref, 1 - u, e, biased_scores(blocks[u + 1], e))
                halves.append(attend(blocks[u], e, u, s_max[e]))
                s_max[e] = next_max
            qs = pl.multiple_of(blocks[u] * NA_Q, NA_Q)
            o_ref[pl.ds(qs, NA_Q), :] = jnp.concatenate(halves, axis=0).T.astype(_BF16)
        return tuple(s_max)

    first = tuple(_issue_scores(s_ref, 0, e, biased_scores(0, e)) for e in range(2))
    lax.fori_loop(0, n_blocks // 2, two_blocks, first)


def _na_attention(h0, na_vt, bias, batch, seq_len):
    h3 = h0.reshape(batch, seq_len, H0_W)
    n_pairs = NA_HEADS // 2
    chunks = seq_len // NA_VT_CHUNK
    return pl.pallas_call(
        functools.partial(_na_kernel, n_rows=seq_len // GRID_W),
        grid=(n_pairs, batch),
        in_specs=[
            pl.BlockSpec((None, seq_len, LANES), lambda g, b: (b, 0, g)),
            pl.BlockSpec((None, seq_len, LANES), lambda g, b: (b, 0, n_pairs + g)),
            pl.BlockSpec((chunks, LANES, NA_VT_CHUNK), lambda g, b: (b, g, 0)),
            pl.BlockSpec((2, 3, NA_KEYS, NA_Q), lambda g, b: (g, 0, 0, 0)),
        ],
        out_specs=pl.BlockSpec((None, seq_len, LANES), lambda g, b: (b, 0, g)),
        out_shape=jax.ShapeDtypeStruct((batch, seq_len, NA_HEADS * HEAD_DIM), _BF16),
        scratch_shapes=[pltpu.VMEM((2, 2, NA_KEYS, NA_Q), _F32)],
        compiler_params=_params("parallel", "parallel"),
        name="na_attention",
    )(h3, h3, na_vt, bias)


def _na_bias_table(rpb):
    cols = np.arange(GRID_W)
    col_start = np.clip(cols - NA_WIN_W // 2, 0, GRID_W - NA_WIN_W)
    kc = np.arange(GRID_W)
    in_cols = (kc[:, None] >= col_start[None, :]) & (kc[:, None] < col_start[None, :] + NA_WIN_W)
    col_idx = kc[:, None] - cols[None, :] + (NA_WIN_W - 1)
    i = np.arange(NA_BLOCK_ROWS)
    j = np.arange(NA_SLAB_ROWS)
    slab_off = np.array([0, -NA_WIN_H // 2, -NA_WIN_H])
    win_off = np.stack([np.zeros_like(i), i - NA_WIN_H // 2, np.full_like(i, -NA_WIN_H // 2)])
    key_row = slab_off[:, None] + j[None, :]
    in_rows = ((key_row[:, :, None] >= win_off[:, None, :])
               & (key_row[:, :, None] < win_off[:, None, :] + NA_WIN_H))
    row_idx = key_row[:, :, None] - i[None, None, :] + (NA_WIN_H - 1)
    row_pick = (in_rows[..., None] & (row_idx[..., None] == np.arange(2 * NA_WIN_H - 1))).astype(np.float32)
    col_pick = (in_cols[..., None] & (col_idx[..., None] == np.arange(2 * NA_WIN_W - 1))).astype(np.float32)
    by_col = jnp.einsum("hab,kcb->hakc", rpb, col_pick, precision=lax.Precision.HIGHEST)
    table = jnp.einsum("ljia,hakc->hljkic", row_pick, by_col, precision=lax.Precision.HIGHEST)
    valid = in_rows[:, :, None, :, None] & in_cols[None, None, :, None, :]
    table = jnp.where(valid[None], table * LOG2E, MASK_VALUE)
    return table.reshape(rpb.shape[0], 3, NA_KEYS, NA_Q).astype(_F32)


def _diff_kernel(q_ref, k_ref, vt_ref, lq1_ref, lk1_ref, lq2_ref, lk2_ref, g_ref, o_ref, s_ref, acc_ref,
                 *, seq_len, lambda_init):
    tq, tk = ATT_TQ, ATT_TK
    n_q, n_k = seq_len // tq, seq_len // tk
    d = 2 * HEAD_DIM
    lam = (jnp.exp(jnp.sum(lq1_ref[...] * lk1_ref[...], axis=1, keepdims=True))
           - jnp.exp(jnp.sum(lq2_ref[...] * lk2_ref[...], axis=1, keepdims=True)) + lambda_init)
    lo64 = lax.broadcasted_iota(jnp.int32, (tk, LANES), 1) < HEAD_DIM
    gain = g_ref[...] * (1.0 - lambda_init)

    def scores(qs, c, i):
        kblk = k_ref[tk * i:tk * (i + 1), :]
        zero = jnp.zeros_like(kblk)
        kk = jnp.where(lo64, kblk, zero) if c == 0 else jnp.where(lo64, zero, kblk)
        return lax.dot_general(kk, q_ref[pl.ds(qs, tq), :], _NT, preferred_element_type=_F32)

    def block(qs, qs_next, chunk_max):
        chunk_max = _attend_block(n_k, 2, lambda c, i: scores(qs, c, i), lambda c: scores(qs_next, c, 0),
                                  lambda c, i: vt_ref[i], s_ref, acc_ref, chunk_max)
        o1 = acc_ref[0, 0:d, :] * (1.0 / acc_ref[0, d:d + 1, :])
        o2 = acc_ref[1, 0:d, :] * (1.0 / acc_ref[1, d:d + 1, :])
        o = (o1 - lam * o2).T
        ms = jnp.mean(o * o, axis=1, keepdims=True)
        o_ref[pl.ds(qs, tq), :] = (o * lax.rsqrt(ms + SUBLN_EPS) * gain).astype(_BF16)
        return chunk_max

    first = tuple(_issue_scores(s_ref, 0, c, scores(0, c, 0)) for c in range(2))
    _for_each_query_block(n_q, tq, block, first)


def _diff_attention(h1, vt1, lq1, lk1, lq2, lk2, g_subln, lambda_init, batch, seq_len):
    h3 = h1.reshape(batch, seq_len, H1_W)
    heads = DIFF_HEADS
    small = lambda n: pl.BlockSpec((1, n), lambda b, h: (0, 0))
    return pl.pallas_call(
        functools.partial(_diff_kernel, seq_len=seq_len, lambda_init=lambda_init),
        grid=(batch, heads),
        in_specs=[
            pl.BlockSpec((None, seq_len, LANES), lambda b, h: (b, 0, h)),
            pl.BlockSpec((None, seq_len, LANES), lambda b, h: (b, 0, heads + h)),
            pl.BlockSpec((seq_len // ATT_TK, LANES, ATT_TK), lambda b, h: (b, h, 0)),
            small(HEAD_DIM), small(HEAD_DIM), small(HEAD_DIM), small(HEAD_DIM), small(2 * HEAD_DIM),
        ],
        out_specs=pl.BlockSpec((None, seq_len, LANES), lambda b, h: (b, 0, h)),
        out_shape=jax.ShapeDtypeStruct((batch, seq_len, D_MODEL), _BF16),
        scratch_shapes=[pltpu.VMEM((2, 2, ATT_TK, ATT_TQ), _F32),
                        pltpu.VMEM((2, 2 * HEAD_DIM + ONES_ROWS, ATT_TQ), _F32)],
        compiler_params=_params("parallel", "parallel"),
        name="diff_attention",
    )(h3, h3, vt1, lq1, lk1, lq2, lk2, g_subln)


def _layer_tail_kernel(*refs, n_parts):
    part_refs = refs[:n_parts]
    (x_ref, wo_ref, g1_ref, b1_ref, wg_ref, wu_ref, wd_ref, g2_ref, b2_ref, o_ref) = refs[n_parts:]
    subs = [slice(ROW_TILE * r, ROW_TILE * (r + 1)) for r in range(x_ref.shape[0] // ROW_TILE)]
    x1 = []
    for rows in subs:
        if n_parts == 1:
            a = part_refs[0][rows, :]
        else:
            a = jnp.concatenate([p[rows, :] for p in part_refs], axis=1)
        mix = jnp.dot(a, wo_ref[...], preferred_element_type=_F32)
        x1.append(_layer_norm(ALPHA * x_ref[rows, :] + mix, g1_ref[...], b1_ref[...]))
    for rows, x1_r in zip(subs, x1):
        xb = x1_r.astype(_BF16)
        hidden = []
        for c in range(N_FF_CHUNKS):
            cols = slice(FF_CHUNK * c, FF_CHUNK * (c + 1))
            gate = jnp.dot(xb, wg_ref[:, cols], preferred_element_type=_F32)
            up = jnp.dot(xb, wu_ref[:, cols], preferred_element_type=_F32)
            hidden.append((gate * jax.nn.sigmoid(gate) * up).astype(_BF16))
        ffn = jnp.dot(jnp.concatenate(hidden, axis=1), wd_ref[...], preferred_element_type=_F32)
        o_ref[rows, :] = _layer_norm(ALPHA * x1_r + ffn, g2_ref[...], b2_ref[...])


def _layer_tail(parts, x2d, w_out, g1, b1, wg, wu, wd, g2, b2):
    rows = x2d.shape[0]
    tm = TAIL_TILE
    part_specs = [pl.BlockSpec((tm, p.shape[1]), lambda i: (i, 0)) for p in parts]
    consts = (w_out, g1, b1, wg, wu, wd, g2, b2)
    return pl.pallas_call(
        functools.partial(_layer_tail_kernel, n_parts=len(parts)),
        grid=(rows // tm,),
        in_specs=part_specs + [pl.BlockSpec((tm, D_MODEL), lambda i: (i, 0))] + [_const_spec(c.shape) for c in consts],
        out_specs=pl.BlockSpec((tm, D_MODEL), lambda i: (i, 0)),
        out_shape=jax.ShapeDtypeStruct((rows, D_MODEL), _F32),
        compiler_params=_params("parallel"),
        name="layer_tail",
    )(*parts, x2d, *consts)


def _rope_tables(angles):
    cos = np.cos(angles)
    sin = np.sin(angles)
    return np.concatenate([cos, cos], axis=1), np.concatenate([-sin, sin], axis=1)


def _inv_freq(dim):
    return ROPE_THETA ** (-np.arange(0, dim, 2, dtype=np.float64) / dim)


def _axial_tables(seq_len):
    t = np.arange(seq_len)
    inv = _inv_freq(HEAD_DIM // 2)
    row_c, row_s = _rope_tables((t // GRID_W)[:, None] * inv[None, :])
    col_c, col_s = _rope_tables((t % GRID_W)[:, None] * inv[None, :])
    cos = np.concatenate([row_c, col_c] * (LANES // HEAD_DIM), axis=1)
    sin = np.concatenate([row_s, col_s] * (LANES // HEAD_DIM), axis=1)
    return cos.astype(np.float32), sin.astype(np.float32)


def _seq_tables(seq_len):
    t = np.arange(seq_len)
    cos, sin = _rope_tables(t[:, None] * _inv_freq(HEAD_DIM)[None, :])
    reps = LANES // HEAD_DIM
    return (np.concatenate([cos] * reps, axis=1).astype(np.float32),
            np.concatenate([sin] * reps, axis=1).astype(np.float32))


def _ffn_weights(wg, wu, wd):
    return wg.astype(_BF16), wu.astype(_BF16), wd.astype(_BF16)


def _trunk(x, p):
    batch, seq_len, _ = x.shape
    x2d = x.reshape(batch * seq_len, D_MODEL)
    row = lambda v: v.reshape(1, -1)

    h0, na_vt, vt0 = _proj0(x2d, p["w_in0_a"], p["w_in0_vt"], p["w_in0_b"], p["ax_cos"][:seq_len],
                            p["ax_sin"][:seq_len], p["gqa_gain"], p["ones_bd"], seq_len)
    a_out = _na_attention(h0, na_vt, p["na_bias"], batch, seq_len).reshape(batch * seq_len, -1)
    b_out = _gqa_attention(h0, vt0, batch, seq_len).reshape(batch * seq_len, -1)
    x2d = _layer_tail([a_out, b_out], x2d, p["w_out0"], row(p["ln_mix_g"][0]), row(p["ln_mix_b"][0]),
                      *p["ffn0"], row(p["ln_ffn_g"][0]), row(p["ln_ffn_b"][0]))

    h1, vt1 = _proj1(x2d, p["w_in1_qk"], p["w_in1_vt"], p["seq_cos"][:seq_len], p["seq_sin"][:seq_len], seq_len)
    lambda_init = 0.8 - 0.6 * math.exp(-0.3 * 1)
    c_out = _diff_attention(h1, vt1, p["lq1"], p["lk1"], p["lq2"], p["lk2"], p["g_subln"],
                            lambda_init, batch, seq_len).reshape(batch * seq_len, -1)
    x2d = _layer_tail([c_out], x2d, p["w_out1"], row(p["ln_mix_g"][1]), row(p["ln_mix_b"][1]),
                      *p["ffn1"], row(p["ln_ffn_g"][1]), row(p["ln_ffn_b"][1]))
    return x2d.reshape(batch, seq_len, D_MODEL)


def kernel(x_prompt, x_sample, w_in_mix0, rpb_na, g_q_gqa, g_k_gqa, w_out_mix0, w_in_mix1, lam_q1, lam_k1,
           lam_q2, lam_k2, g_subln, w_out_mix1, ln_mix_g, ln_mix_b, w_ffn_gate, w_ffn_up, w_ffn_down,
           ln_ffn_g, ln_ffn_b):
    max_len = max(x_prompt.shape[1], x_sample.shape[1])
    ax_cos, ax_sin = _axial_tables(max_len)
    seq_cos, seq_sin = _seq_tables(max_len)
    head_of_lane = jnp.arange(LANES) // HEAD_DIM
    w0 = w_in_mix0[0].astype(_BF16)
    na_w = NA_HEADS * HEAD_DIM
    p = {
        "w_in0_a": w0[:, :2 * na_w],
        "w_in0_vt": w0[:, 2 * na_w:3 * na_w].T,
        "w_in0_b": w0[:, 3 * na_w:],
        "ax_cos": ax_cos, "ax_sin": ax_sin, "seq_cos": seq_cos, "seq_sin": seq_sin,
        "gqa_gain": jnp.concatenate([jnp.tile(g_q_gqa[0], GQA_HEADS), jnp.tile(g_k_gqa[0], GQA_KV_HEADS)]
                                    ).reshape(1, -1).astype(_F32),
        "ones_bd": (head_of_lane[:, None] == head_of_lane[None, :]).astype(_BF16),
        "na_bias": _na_bias_table(rpb_na[0]),
        "w_out0": w_out_mix0[0].astype(_BF16),
        "ffn0": _ffn_weights(w_ffn_gate[0], w_ffn_up[0], w_ffn_down[0]),
        "w_in1_qk": w_in_mix1[0][:, :2 * D_MODEL].astype(_BF16),
        "w_in1_vt": w_in_mix1[0][:, 2 * D_MODEL:].T.astype(_BF16),
        "lq1": lam_q1[0].reshape(1, -1), "lk1": lam_k1[0].reshape(1, -1),
        "lq2": lam_q2[0].reshape(1, -1), "lk2": lam_k2[0].reshape(1, -1),
        "g_subln": g_subln[0].reshape(1, -1),
        "w_out1": w_out_mix1[0].astype(_BF16),
        "ffn1": _ffn_weights(w_ffn_gate[1], w_ffn_up[1], w_ffn_down[1]),
        "ln_mix_g": ln_mix_g, "ln_mix_b": ln_mix_b, "ln_ffn_g": ln_ffn_g, "ln_ffn_b": ln_ffn_b,
    }
    return _trunk(x_prompt, p), _trunk(x_sample, p)
```

```python
import functools
import math

import jax
import jax.numpy as jnp
import numpy as np
from jax import lax
from jax.experimental import pallas as pl
from jax.experimental.pallas import tpu as pltpu

D_MODEL = 1024
DEPTH = 2
GRID_W = 64
HEAD_DIM = 64
NA_HEADS = 8
GQA_HEADS = 8
GQA_KV_HEADS = 2
DIFF_HEADS = 8
NA_WIN_H = 8
NA_WIN_W = 16
ROPE_THETA = 10000.0
D_FF = 2816
LN_EPS = 1e-5
RMS_EPS = 1e-6
SUBLN_EPS = 1e-5
ALPHA = (2.0 * DEPTH) ** 0.25
Q_SCALE = HEAD_DIM ** -0.5
LOG2E = math.log2(math.e)

LANES = 128
FF_CHUNK = 256
N_FF_CHUNKS = D_FF // FF_CHUNK
ROW_TILE = 512
TAIL_TILE = 1024
ATT_TQ = 512
ATT_TK = 512
ONES_ROWS = 16
DIFF_PAD_ROWS = 32
assert ATT_TK == ROW_TILE
MASK_VALUE = -1e30
VMEM_LIMIT = 56 * 1024 * 1024

_BF16 = jnp.bfloat16
_F32 = jnp.float32
_NT = (((1,), (1,)), ((), ()))


def _params(*semantics):
    return pltpu.CompilerParams(dimension_semantics=semantics, vmem_limit_bytes=VMEM_LIMIT)


def _const_spec(shape):
    zeros = (0,) * len(shape)
    return pl.BlockSpec(shape, lambda *_: zeros, pipeline_mode=pl.Buffered(1))


def _layer_norm(xf, gain, bias):
    mu = jnp.mean(xf, axis=-1, keepdims=True)
    xc = xf - mu
    var = jnp.mean(xc * xc, axis=-1, keepdims=True)
    return xc * lax.rsqrt(var + LN_EPS) * gain + bias


def _rotate_pairs(x, half):
    lane = lax.broadcasted_iota(jnp.int32, x.shape, 1)
    first = (lane % (2 * half)) < half
    return jnp.where(first, pltpu.roll(x, LANES - half, 1), pltpu.roll(x, half, 1))


H0_W = 2048
NA_VT_CHUNK = 256


def _proj0_kernel(x_ref, wa_ref, wvt_ref, wb_ref, cos_ref, sin_ref, gain_ref, ones_ref, o_ref, na_vt_ref, vt_ref):
    xb = x_ref[...].astype(_BF16)
    hb = jnp.dot(xb, wb_ref[...], preferred_element_type=_F32)
    cos = cos_ref[...]
    sin = sin_ref[...]
    ones_bd = ones_ref[...]
    lo64 = lax.broadcasted_iota(jnp.int32, cos.shape, 1) < HEAD_DIM

    def norm_rope(xg, gain):
        ssq = jnp.dot((xg * xg).astype(_BF16), ones_bd, preferred_element_type=_F32)
        n = xg * lax.rsqrt(ssq * (1.0 / HEAD_DIM) + RMS_EPS) * gain
        return n * cos + _rotate_pairs(n, 16) * sin

    for g in range(4):
        sl = slice(LANES * g, LANES * (g + 1))
        qg = norm_rope(hb[:, sl], gain_ref[:, sl]) * (Q_SCALE * LOG2E)
        o_ref[:, 1024 + LANES * g:1024 + LANES * (g + 1)] = qg.astype(_BF16)

    k = norm_rope(hb[:, 512:640], gain_ref[:, 512:640])
    kr = pltpu.roll(k, HEAD_DIM, 1)
    zero = jnp.zeros_like(k)
    parts = (jnp.where(lo64, k, zero), jnp.where(lo64, zero, kr),
             jnp.where(lo64, kr, zero), jnp.where(lo64, zero, k))
    for i, part in enumerate(parts):
        o_ref[:, 1536 + LANES * i:1536 + LANES * (i + 1)] = part.astype(_BF16)
    vt_ref[...] = hb[:, 640:768].T.astype(_BF16)

    ha = jnp.dot(xb, wa_ref[...], preferred_element_type=_F32)
    o_ref[:, 0:512] = (ha[:, 0:512] * (Q_SCALE * LOG2E)).astype(_BF16)
    o_ref[:, 512:1024] = ha[:, 512:1024].astype(_BF16)
    na_vt = lax.dot_general(wvt_ref[...], xb, _NT, preferred_element_type=_F32).astype(_BF16)
    for t in range(na_vt_ref.shape[0]):
        na_vt_ref[t] = na_vt[:, NA_VT_CHUNK * t:NA_VT_CHUNK * (t + 1)]


def _proj0(x2d, wa, wvt, wb, cos, sin, gain, ones_bd, seq_len):
    rows = x2d.shape[0]
    tm = ROW_TILE
    pos_blocks = seq_len // tm
    na_chunks = tm // NA_VT_CHUNK
    na_w = NA_HEADS * HEAD_DIM
    return pl.pallas_call(
        _proj0_kernel,
        grid=(rows // tm,),
        in_specs=[
            pl.BlockSpec((tm, D_MODEL), lambda i: (i, 0)),
            _const_spec(wa.shape),
            _const_spec(wvt.shape),
            _const_spec(wb.shape),
            pl.BlockSpec((tm, LANES), lambda i: (i % pos_blocks, 0)),
            pl.BlockSpec((tm, LANES), lambda i: (i % pos_blocks, 0)),
            _const_spec(gain.shape),
            _const_spec(ones_bd.shape),
        ],
        out_specs=[pl.BlockSpec((tm, H0_W), lambda i: (i, 0)),
                   pl.BlockSpec((na_chunks, na_w, NA_VT_CHUNK), lambda i: (i, 0, 0)),
                   pl.BlockSpec((None, LANES, tm), lambda i: (i, 0, 0))],
        out_shape=[jax.ShapeDtypeStruct((rows, H0_W), _BF16),
                   jax.ShapeDtypeStruct((rows // NA_VT_CHUNK, na_w, NA_VT_CHUNK), _BF16),
                   jax.ShapeDtypeStruct((rows // tm, LANES, tm), _BF16)],
        compiler_params=_params("parallel"),
        name="proj0",
    )(x2d, wa, wvt, wb, cos, sin, gain, ones_bd)


H1_W = 2 * D_MODEL


def _proj1_kernel(x_ref, w_ref, wvt_ref, cos_ref, sin_ref, o_ref, vt_ref):
    xb = x_ref[...].astype(_BF16)
    cos = cos_ref[...]
    sin = sin_ref[...]
    for part, scale in ((0, Q_SCALE * LOG2E), (1, 1.0)):
        base = part * D_MODEL
        h = jnp.dot(xb, w_ref[:, base:base + D_MODEL], preferred_element_type=_F32)
        for g in range(D_MODEL // LANES):
            xg = h[:, LANES * g:LANES * (g + 1)]
            r = xg * cos + _rotate_pairs(xg, 32) * sin
            if scale != 1.0:
                r = r * scale
            o_ref[:, base + LANES * g:base + LANES * (g + 1)] = r.astype(_BF16)
    vt_ref[...] = lax.dot_general(wvt_ref[...], xb, _NT, preferred_element_type=_F32).astype(_BF16)


def _proj1(x2d, w_qk, w_vt, cos, sin, seq_len):
    rows = x2d.shape[0]
    tm = ROW_TILE
    pos_blocks = seq_len // tm
    return pl.pallas_call(
        _proj1_kernel,
        grid=(rows // tm,),
        in_specs=[
            pl.BlockSpec((tm, D_MODEL), lambda i: (i, 0)),
            _const_spec(w_qk.shape),
            _const_spec(w_vt.shape),
            pl.BlockSpec((tm, LANES), lambda i: (i % pos_blocks, 0)),
            pl.BlockSpec((tm, LANES), lambda i: (i % pos_blocks, 0)),
        ],
        out_specs=[pl.BlockSpec((tm, H1_W), lambda i: (i, 0)),
                   pl.BlockSpec((None, D_MODEL, tm), lambda i: (i, 0, 0))],
        out_shape=[jax.ShapeDtypeStruct((rows, H1_W), _BF16),
                   jax.ShapeDtypeStruct((rows // tm, D_MODEL, tm), _BF16)],
        compiler_params=_params("parallel"),
        name="proj1",
    )(x2d, w_qk, w_vt, cos, sin)


def _with_ones(vt):
    return jnp.concatenate([vt, jnp.ones((ONES_ROWS, vt.shape[1]), vt.dtype)], axis=0)


def _issue_scores(s_ref, slot, c, s):
    s_ref[slot, c] = s
    return jnp.max(s, axis=0, keepdims=True)


def _attend_block(n_k, n_chain, scores_fn, next_scores_fn, vt_fn, s_ref, acc_ref, chunk_max):
    chunk_max = list(chunk_max)
    m_state = [None] * n_chain
    for i in range(n_k):
        slot = i % 2
        for c in range(n_chain):
            s_next = scores_fn(c, i + 1) if i + 1 < n_k else next_scores_fn(c)
            next_max = _issue_scores(s_ref, 1 - slot, c, s_next)
            m_next = chunk_max[c] if i == 0 else jnp.maximum(m_state[c], chunk_max[c])
            p = jnp.exp2(s_ref[slot, c] - m_next).astype(_BF16)
            pv = jnp.dot(_with_ones(vt_fn(c, i)), p, preferred_element_type=_F32)
            acc_ref[c] = pv if i == 0 else jnp.exp2(m_state[c] - m_next) * acc_ref[c] + pv
            m_state[c] = m_next
            chunk_max[c] = next_max
    return tuple(chunk_max)


def _for_each_query_block(n_q, tq, block, first_max):
    def body(qi, chunk_max):
        qs = pl.multiple_of(qi * tq, tq)
        qs_next = pl.multiple_of(jnp.minimum(qi + 1, n_q - 1) * tq, tq)
        return block(qs, qs_next, chunk_max)

    lax.fori_loop(0, n_q, body, first_max, unroll=2)


def _gqa_kernel(q_ref, k_ref, vt_ref, o_ref, s_ref, acc_ref, *, seq_len):
    tq, tk = ATT_TQ, ATT_TK
    n_q, n_k = seq_len // tq, seq_len // tk
    n_pairs = GQA_HEADS // 2
    d = HEAD_DIM

    def scores(qs, pair, e, i):
        kvh = pair // (n_pairs // GQA_KV_HEADS)
        col = 2 * LANES * kvh + LANES * e
        q = q_ref[pl.ds(qs, tq), LANES * pair:LANES * (pair + 1)]
        return lax.dot_general(k_ref[tk * i:tk * (i + 1), col:col + LANES], q, _NT, preferred_element_type=_F32)

    def q_body(qi, chunk_max):
        qs = pl.multiple_of(qi * tq, tq)
        qs_next = pl.multiple_of(jnp.minimum(qi + 1, n_q - 1) * tq, tq)
        for pair in range(n_pairs):
            kvh = pair // (n_pairs // GQA_KV_HEADS)
            nxt = (qs, pair + 1) if pair + 1 < n_pairs else (qs_next, 0)
            chunk_max = _attend_block(
                n_k, 2,
                lambda e, i, pair=pair: scores(qs, pair, e, i),
                lambda e, nxt=nxt: scores(nxt[0], nxt[1], e, 0),
                lambda e, i: vt_ref[i],
                s_ref, acc_ref, chunk_max)
            halves = [acc_ref[e, d * kvh:d * (kvh + 1), :] * (1.0 / acc_ref[e, 2 * d:2 * d + 1, :])
                      for e in range(2)]
            out = jnp.concatenate(halves, axis=0).T
            o_ref[pl.ds(qs, tq), LANES * pair:LANES * (pair + 1)] = out.astype(_BF16)
        return chunk_max

    first = tuple(_issue_scores(s_ref, 0, e, scores(0, 0, e, 0)) for e in range(2))
    lax.fori_loop(0, n_q, q_body, first)


def _gqa_attention(h0, vt0, batch, seq_len):
    h3 = h0.reshape(batch, seq_len, H0_W)
    width = GQA_HEADS * HEAD_DIM
    return pl.pallas_call(
        functools.partial(_gqa_kernel, seq_len=seq_len),
        grid=(batch,),
        in_specs=[
            pl.BlockSpec((None, seq_len, width), lambda b: (b, 0, 2)),
            pl.BlockSpec((None, seq_len, width), lambda b: (b, 0, 3)),
            pl.BlockSpec((seq_len // ATT_TK, LANES, ATT_TK), lambda b: (b, 0, 0)),
        ],
        out_specs=pl.BlockSpec((None, seq_len, width), lambda b: (b, 0, 0)),
        out_shape=jax.ShapeDtypeStruct((batch, seq_len, width), _BF16),
        scratch_shapes=[pltpu.VMEM((2, 2, ATT_TK, ATT_TQ), _F32),
                        pltpu.VMEM((2, GQA_KV_HEADS * HEAD_DIM + ONES_ROWS, ATT_TQ), _F32)],
        compiler_params=_params("parallel"),
        name="gqa_attention",
    )(h3, h3, vt0)


NA_BLOCK_ROWS = 4
NA_SLAB_ROWS = NA_WIN_H + NA_BLOCK_ROWS
NA_Q = NA_BLOCK_ROWS * GRID_W
NA_KEYS = NA_SLAB_ROWS * GRID_W
assert NA_Q == NA_VT_CHUNK and NA_KEYS % NA_VT_CHUNK == 0


def _na_kernel(q_ref, k_ref, vt_ref, bias_ref, o_ref, s_ref, *, n_rows):
    n_blocks = n_rows // NA_BLOCK_ROWS
    lo64 = lax.broadcasted_iota(jnp.int32, (NA_KEYS, LANES), 1) < HEAD_DIM
    d = HEAD_DIM

    def slab_start(bi):
        return jnp.clip(bi - 1, 0, n_blocks - NA_SLAB_ROWS // NA_BLOCK_ROWS)

    def biased_scores(bi, e):
        q = q_ref[pl.ds(pl.multiple_of(bi * NA_Q, NA_Q), NA_Q), :]
        kslab = k_ref[pl.ds(pl.multiple_of(slab_start(bi) * NA_Q, NA_Q), NA_KEYS), :]
        zero = jnp.zeros_like(kslab)
        kk = jnp.where(lo64, kslab, zero) if e == 0 else jnp.where(lo64, zero, kslab)
        layout = jnp.where(bi == 0, 0, jnp.where(bi == n_blocks - 1, 2, 1))
        return lax.dot_general(kk, q, _NT, preferred_element_type=_F32) + bias_ref[e, layout]

    def attend(bi, e, slot, s_max):
        p = jnp.exp2(s_ref[slot, e] - s_max).astype(_BF16)
        vt = jnp.concatenate([vt_ref[slab_start(bi) + t] for t in range(NA_KEYS // NA_VT_CHUNK)],
                             axis=1)
        acc = jnp.dot(_with_ones(vt), p, preferred_element_type=_F32)
        return acc[d * e:d * (e + 1), :] * (1.0 / acc[2 * d:2 * d + 1, :])

    def two_blocks(j, s_max):
        b0 = 2 * j
        blocks = [b0, b0 + 1, jnp.minimum(b0 + 2, n_blocks - 1)]
        s_max = list(s_max)
        for u in range(2):
            halves = []
            for e in range(2):
                next_max = _issue_scores(s_ref, 1 - u, e, biased_scores(blocks[u + 1], e))
                halves.append(attend(blocks[u], e, u, s_max[e]))
                s_max[e] = next_max
            qs = pl.multiple_of(blocks[u] * NA_Q, NA_Q)
            o_ref[pl.ds(qs, NA_Q), :] = jnp.concatenate(halves, axis=0).T.astype(_BF16)
        return tuple(s_max)

    first = tuple(_issue_scores(s_ref, 0, e, biased_scores(0, e)) for e in range(2))
    lax.fori_loop(0, n_blocks // 2, two_blocks, first)


def _na_attention(h0, na_vt, bias, batch, seq_len):
    h3 = h0.reshape(batch, seq_len, H0_W)
    n_pairs = NA_HEADS // 2
    chunks = seq_len // NA_VT_CHUNK
    return pl.pallas_call(
        functools.partial(_na_kernel, n_rows=seq_len // GRID_W),
        grid=(n_pairs, batch),
        in_specs=[
            pl.BlockSpec((None, seq_len, LANES), lambda g, b: (b, 0, g)),
            pl.BlockSpec((None, seq_len, LANES), lambda g, b: (b, 0, n_pairs + g)),
            pl.BlockSpec((chunks, LANES, NA_VT_CHUNK), lambda g, b: (b, g, 0)),
            pl.BlockSpec((2, 3, NA_KEYS, NA_Q), lambda g, b: (g, 0, 0, 0)),
        ],
        out_specs=pl.BlockSpec((None, seq_len, LANES), lambda g, b: (b, 0, g)),
        out_shape=jax.ShapeDtypeStruct((batch, seq_len, NA_HEADS * HEAD_DIM), _BF16),
        scratch_shapes=[pltpu.VMEM((2, 2, NA_KEYS, NA_Q), _F32)],
        compiler_params=_params("parallel", "parallel"),
        name="na_attention",
    )(h3, h3, na_vt, bias)


def _na_bias_table(rpb):
    cols = np.arange(GRID_W)
    col_start = np.clip(cols - NA_WIN_W // 2, 0, GRID_W - NA_WIN_W)
    kc = np.arange(GRID_W)
    in_cols = (kc[:, None] >= col_start[None, :]) & (kc[:, None] < col_start[None, :] + NA_WIN_W)
    col_idx = kc[:, None] - cols[None, :] + (NA_WIN_W - 1)
    i = np.arange(NA_BLOCK_ROWS)
    j = np.arange(NA_SLAB_ROWS)
    slab_off = np.array([0, -NA_WIN_H // 2, -NA_WIN_H])
    win_off = np.stack([np.zeros_like(i), i - NA_WIN_H // 2, np.full_like(i, -NA_WIN_H // 2)])
    key_row = slab_off[:, None] + j[None, :]
    in_rows = ((key_row[:, :, None] >= win_off[:, None, :])
               & (key_row[:, :, None] < win_off[:, None, :] + NA_WIN_H))
    row_idx = key_row[:, :, None] - i[None, None, :] + (NA_WIN_H - 1)
    row_pick = (in_rows[..., None] & (row_idx[..., None] == np.arange(2 * NA_WIN_H - 1))).astype(np.float32)
    col_pick = (in_cols[..., None] & (col_idx[..., None] == np.arange(2 * NA_WIN_W - 1))).astype(np.float32)
    by_col = jnp.einsum("hab,kcb->hakc", rpb, col_pick, precision=lax.Precision.HIGHEST)
    table = jnp.einsum("ljia,hakc->hljkic", row_pick, by_col, precision=lax.Precision.HIGHEST)
    valid = in_rows[:, :, None, :, None] & in_cols[None, None, :, None, :]
    table = jnp.where(valid[None], table * LOG2E, MASK_VALUE)
    return table.reshape(rpb.shape[0], 3, NA_KEYS, NA_Q).astype(_F32)


def _diff_kernel(q_ref, k_ref, vt_ref, lq1_ref, lk1_ref, lq2_ref, lk2_ref, g_ref, o_ref, s_ref, acc_ref,
                 *, seq_len, lambda_init):
    tq, tk = ATT_TQ, ATT_TK
    n_q, n_k = seq_len // tq, seq_len // tk
    d = 2 * HEAD_DIM
    lam = (jnp.exp(jnp.sum(lq1_ref[...] * lk1_ref[...], axis=1, keepdims=True))
           - jnp.exp(jnp.sum(lq2_ref[...] * lk2_ref[...], axis=1, keepdims=True)) + lambda_init)
    lo64 = lax.broadcasted_iota(jnp.int32, (tk, LANES), 1) < HEAD_DIM
    gain = g_ref[...] * (1.0 - lambda_init)

    def scores(qs, c, i):
        kblk = k_ref[tk * i:tk * (i + 1), :]
        zero = jnp.zeros_like(kblk)
        kk = jnp.where(lo64, kblk, zero) if c == 0 else jnp.where(lo64, zero, kblk)
        return lax.dot_general(kk, q_ref[pl.ds(qs, tq), :], _NT, preferred_element_type=_F32)

    def vt_ext(i):
        vt = vt_ref[i]
        return jnp.concatenate([vt, jnp.ones((DIFF_PAD_ROWS, vt.shape[1]), vt.dtype)], axis=0)

    def block(qs, qs_next, chunk_max):
        chunk_max = _attend_block(n_k, 2, lambda c, i: scores(qs, c, i), lambda c: scores(qs_next, c, 0),
                                  lambda c, i: vt_ext(i), s_ref, acc_ref, chunk_max)
        o1 = acc_ref[0, 0:d, :] * (1.0 / acc_ref[0, d:d + 1, :])
        o2 = acc_ref[1, 0:d, :] * (1.0 / acc_ref[1, d:d + 1, :])
        o = (o1 - lam * o2).T
        ms = jnp.mean(o * o, axis=1, keepdims=True)
        o_ref[pl.ds(qs, tq), :] = (o * lax.rsqrt(ms + SUBLN_EPS) * gain).astype(_BF16)
        return chunk_max

    first = tuple(_issue_scores(s_ref, 0, c, scores(0, c, 0)) for c in range(2))
    _for_each_query_block(n_q, tq, block, first)


def _diff_attention(h1, vt1, lq1, lk1, lq2, lk2, g_subln, lambda_init, batch, seq_len):
    h3 = h1.reshape(batch, seq_len, H1_W)
    heads = DIFF_HEADS
    small = lambda n: pl.BlockSpec((1, n), lambda b, h: (0, 0))
    return pl.pallas_call(
        functools.partial(_diff_kernel, seq_len=seq_len, lambda_init=lambda_init),
        grid=(batch, heads),
        in_specs=[
            pl.BlockSpec((None, seq_len, LANES), lambda b, h: (b, 0, h)),
            pl.BlockSpec((None, seq_len, LANES), lambda b, h: (b, 0, heads + h)),
            pl.BlockSpec((seq_len // ATT_TK, LANES, ATT_TK), lambda b, h: (b, h, 0)),
            small(HEAD_DIM), small(HEAD_DIM), small(HEAD_DIM), small(HEAD_DIM), small(2 * HEAD_DIM),
        ],
        out_specs=pl.BlockSpec((None, seq_len, LANES), lambda b, h: (b, 0, h)),
        out_shape=jax.ShapeDtypeStruct((batch, seq_len, D_MODEL), _BF16),
        scratch_shapes=[pltpu.VMEM((2, 2, ATT_TK, ATT_TQ), _F32),
                        pltpu.VMEM((2, 2 * HEAD_DIM + DIFF_PAD_ROWS + ONES_ROWS, ATT_TQ), _F32)],
        compiler_params=_params("parallel", "parallel"),
        name="diff_attention",
    )(h3, h3, vt1, lq1, lk1, lq2, lk2, g_subln)


def _layer_tail_kernel(*refs, n_parts):
    part_refs = refs[:n_parts]
    (x_ref, wo_ref, g1_ref, b1_ref, wg_ref, wu_ref, wd_ref, g2_ref, b2_ref, o_ref) = refs[n_parts:]
    subs = [slice(ROW_TILE * r, ROW_TILE * (r + 1)) for r in range(x_ref.shape[0] // ROW_TILE)]
    x1 = []
    for rows in subs:
        if n_parts == 1:
            a = part_refs[0][rows, :]
        else:
            a = jnp.concatenate([p[rows, :] for p in part_refs], axis=1)
        mix = jnp.dot(a, wo_ref[...], preferred_element_type=_F32)
        x1.append(_layer_norm(ALPHA * x_ref[rows, :] + mix, g1_ref[...], b1_ref[...]))
    for rows, x1_r in zip(subs, x1):
        xb = x1_r.astype(_BF16)
        hidden = []
        for c in range(N_FF_CHUNKS):
            cols = slice(FF_CHUNK * c, FF_CHUNK * (c + 1))
            gate = jnp.dot(xb, wg_ref[:, cols], preferred_element_type=_F32)
            up = jnp.dot(xb, wu_ref[:, cols], preferred_element_type=_F32)
            hidden.append((gate * jax.nn.sigmoid(gate) * up).astype(_BF16))
        ffn = jnp.dot(jnp.concatenate(hidden, axis=1), wd_ref[...], preferred_element_type=_F32)
        o_ref[rows, :] = _layer_norm(ALPHA * x1_r + ffn, g2_ref[...], b2_ref[...])


def _layer_tail(parts, x2d, w_out, g1, b1, wg, wu, wd, g2, b2):
    rows = x2d.shape[0]
    tm = TAIL_TILE
    part_specs = [pl.BlockSpec((tm, p.shape[1]), lambda i: (i, 0)) for p in parts]
    consts = (w_out, g1, b1, wg, wu, wd, g2, b2)
    return pl.pallas_call(
        functools.partial(_layer_tail_kernel, n_parts=len(parts)),
        grid=(rows // tm,),
        in_specs=part_specs + [pl.BlockSpec((tm, D_MODEL), lambda i: (i, 0))] + [_const_spec(c.shape) for c in consts],
        out_specs=pl.BlockSpec((tm, D_MODEL), lambda i: (i, 0)),
        out_shape=jax.ShapeDtypeStruct((rows, D_MODEL), _F32),
        compiler_params=_params("parallel"),
        name="layer_tail",
    )(*parts, x2d, *consts)


def _rope_tables(angles):
    cos = np.cos(angles)
    sin = np.sin(angles)
    return np.concatenate([cos, cos], axis=1), np.concatenate([-sin, sin], axis=1)


def _inv_freq(dim):
    return ROPE_THETA ** (-np.arange(0, dim, 2, dtype=np.float64) / dim)


def _axial_tables(seq_len):
    t = np.arange(seq_len)
    inv = _inv_freq(HEAD_DIM // 2)
    row_c, row_s = _rope_tables((t // GRID_W)[:, None] * inv[None, :])
    col_c, col_s = _rope_tables((t % GRID_W)[:, None] * inv[None, :])
    cos = np.concatenate([row_c, col_c] * (LANES // HEAD_DIM), axis=1)
    sin = np.concatenate([row_s, col_s] * (LANES // HEAD_DIM), axis=1)
    return cos.astype(np.float32), sin.astype(np.float32)


def _seq_tables(seq_len):
    t = np.arange(seq_len)
    cos, sin = _rope_tables(t[:, None] * _inv_freq(HEAD_DIM)[None, :])
    reps = LANES // HEAD_DIM
    return (np.concatenate([cos] * reps, axis=1).astype(np.float32),
            np.concatenate([sin] * reps, axis=1).astype(np.float32))


def _ffn_weights(wg, wu, wd):
    return wg.astype(_BF16), wu.astype(_BF16), wd.astype(_BF16)


def _trunk(x, p):
    batch, seq_len, _ = x.shape
    x2d = x.reshape(batch * seq_len, D_MODEL)
    row = lambda v: v.reshape(1, -1)

    h0, na_vt, vt0 = _proj0(x2d, p["w_in0_a"], p["w_in0_vt"], p["w_in0_b"], p["ax_cos"][:seq_len],
                            p["ax_sin"][:seq_len], p["gqa_gain"], p["ones_bd"], seq_len)
    a_out = _na_attention(h0, na_vt, p["na_bias"], batch, seq_len).reshape(batch * seq_len, -1)
    b_out = _gqa_attention(h0, vt0, batch, seq_len).reshape(batch * seq_len, -1)
    x2d = _layer_tail([a_out, b_out], x2d, p["w_out0"], row(p["ln_mix_g"][0]), row(p["ln_mix_b"][0]),
                      *p["ffn0"], row(p["ln_ffn_g"][0]), row(p["ln_ffn_b"][0]))

    h1, vt1 = _proj1(x2d, p["w_in1_qk"], p["w_in1_vt"], p["seq_cos"][:seq_len], p["seq_sin"][:seq_len], seq_len)
    lambda_init = 0.8 - 0.6 * math.exp(-0.3 * 1)
    c_out = _diff_attention(h1, vt1, p["lq1"], p["lk1"], p["lq2"], p["lk2"], p["g_subln"],
                            lambda_init, batch, seq_len).reshape(batch * seq_len, -1)
    x2d = _layer_tail([c_out], x2d, p["w_out1"], row(p["ln_mix_g"][1]), row(p["ln_mix_b"][1]),
                      *p["ffn1"], row(p["ln_ffn_g"][1]), row(p["ln_ffn_b"][1]))
    return x2d.reshape(batch, seq_len, D_MODEL)


def kernel(x_prompt, x_sample, w_in_mix0, rpb_na, g_q_gqa, g_k_gqa, w_out_mix0, w_in_mix1, lam_q1, lam_k1,
           lam_q2, lam_k2, g_subln, w_out_mix1, ln_mix_g, ln_mix_b, w_ffn_gate, w_ffn_up, w_ffn_down,
           ln_ffn_g, ln_ffn_b):
    max_len = max(x_prompt.shape[1], x_sample.shape[1])
    ax_cos, ax_sin = _axial_tables(max_len)
    seq_cos, seq_sin = _seq_tables(max_len)
    head_of_lane = jnp.arange(LANES) // HEAD_DIM
    w0 = w_in_mix0[0].astype(_BF16)
    na_w = NA_HEADS * HEAD_DIM
    p = {
        "w_in0_a": w0[:, :2 * na_w],
        "w_in0_vt": w0[:, 2 * na_w:3 * na_w].T,
        "w_in0_b": w0[:, 3 * na_w:],
        "ax_cos": ax_cos, "ax_sin": ax_sin, "seq_cos": seq_cos, "seq_sin": seq_sin,
        "gqa_gain": jnp.concatenate([jnp.tile(g_q_gqa[0], GQA_HEADS), jnp.tile(g_k_gqa[0], GQA_KV_HEADS)]
                                    ).reshape(1, -1).astype(_F32),
        "ones_bd": (head_of_lane[:, None] == head_of_lane[None, :]).astype(_BF16),
        "na_bias": _na_bias_table(rpb_na[0]),
        "w_out0": w_out_mix0[0].astype(_BF16),
        "ffn0": _ffn_weights(w_ffn_gate[0], w_ffn_up[0], w_ffn_down[0]),
        "w_in1_qk": w_in_mix1[0][:, :2 * D_MODEL].astype(_BF16),
        "w_in1_vt": w_in_mix1[0][:, 2 * D_MODEL:].T.astype(_BF16),
        "lq1": lam_q1[0].reshape(1, -1), "lk1": lam_k1[0].reshape(1, -1),
        "lq2": lam_q2[0].reshape(1, -1), "lk2": lam_k2[0].reshape(1, -1),
        "g_subln": g_subln[0].reshape(1, -1),
        "w_out1": w_out_mix1[0].astype(_BF16),
        "ffn1": _ffn_weights(w_ffn_gate[1], w_ffn_up[1], w_ffn_down[1]),
        "ln_mix_g": ln_mix_g, "ln_mix_b": ln_mix_b, "ln_ffn_g": ln_ffn_g, "ln_ffn_b": ln_ffn_b,
    }
    return _trunk(x_prompt, p), _trunk(x_sample, p)
```
